```python
import math
import jax, jax.numpy as jnp
from jax import lax
import numpy as np

D_MODEL = 1024
BATCH = 4
SEQ = 8192
DEPTH = 2
DEC_BATCH = 32
DEC_SEQ = 1
PAST_LEN = 16384
PAGE_SIZE = 128

D_CONV = D_MODEL // 2
D_POOL = D_MODEL // 2
CONV_WIDTH = 3
POOL_WINDOWS = (2, 4, 8, 16)
POOL_GROUPS = len(POOL_WINDOWS)
POOL_GC = D_POOL // POOL_GROUPS
POOL_BUF = max(POOL_WINDOWS) - 1
HEAD_DIM = 64
N_HEADS = D_MODEL // (2 * HEAD_DIM)
V_DIM = 2 * HEAD_DIM
D_FF = 4 * D_MODEL
ROPE_THETA = 10000.0
NORM_EPS = 1e-6
Q_BLOCK = 128
N_EVEN = (DEPTH + 1) // 2
N_ODD = DEPTH // 2
NEG_INF = -1e30

kernel_name = 'hybrid_conv_pool_diffattn_step'


def rms_norm(x, g):
    xf = x.astype(jnp.float32)
    y = xf * lax.rsqrt(jnp.mean(xf * xf, axis=-1, keepdims=True) + NORM_EPS)
    return (y * g.astype(jnp.float32)).astype(x.dtype)


def rope(x, pos):
    half = HEAD_DIM // 2
    inv = 1.0 / (ROPE_THETA ** (jnp.arange(half, dtype=jnp.float32) / half))
    ang = pos.astype(jnp.float32)[:, None] * inv[None, :]
    cos = jnp.cos(ang)[:, None, None, :]
    sin = jnp.sin(ang)[:, None, None, :]
    xf = x.astype(jnp.float32)
    x1, x2 = xf[..., :half], xf[..., half:]
    return jnp.concatenate([x1 * cos - x2 * sin, x2 * cos + x1 * sin], axis=-1).astype(x.dtype)


def sq_relu_mlp(x, w_up, w_down):
    h = jax.nn.relu(x @ w_up)
    return (h * h) @ w_down


def conv_pool_mixer(h, conv_buf, pool_buf, pos, w_in, conv_w, pool_w, pool_scale, w_out):
    b, t, _ = h.shape
    proj = h @ w_in
    b_gate = proj[..., :D_CONV]
    c_gate = proj[..., D_CONV:2 * D_CONV]
    hc = proj[..., 2 * D_CONV:3 * D_CONV]
    u = proj[..., 3 * D_CONV:]
    conv_in = c_gate * hc
    conv_ext = jnp.concatenate([conv_buf.astype(conv_in.dtype), conv_in], axis=1)
    conv_out = conv_w[0] * conv_ext[:, 0:t]
    for k in range(1, CONV_WIDTH):
        conv_out = conv_out + conv_w[k] * conv_ext[:, k:k + t]
    y_a = b_gate * conv_out
    u_ext = jnp.concatenate([pool_buf.astype(u.dtype), u], axis=1)
    uf = u_ext.astype(jnp.float32).reshape(b, POOL_BUF + t, POOL_GROUPS, POOL_GC)
    cs = jnp.concatenate([jnp.zeros((b, 1, POOL_GROUPS, POOL_GC), jnp.float32), jnp.cumsum(uf, axis=1)], axis=1)
    hi = cs[:, POOL_BUF + 1:POOL_BUF + 1 + t]
    diffs = []
    for g, w in enumerate(POOL_WINDOWS):
        lo = cs[:, POOL_BUF + 1 - w:POOL_BUF + 1 - w + t, g]
        cnt = jnp.minimum(pos + 1, w).astype(jnp.float32)[None, :, None]
        diffs.append((hi[:, :, g] - lo) / cnt - uf[:, POOL_BUF:, g])
    d = jnp.stack(diffs, axis=2)
    y_b = jnp.einsum('btgc,gce->btge', d, pool_w.astype(jnp.float32)).reshape(b, t, D_POOL)
    y_b = (y_b * pool_scale.astype(jnp.float32)).astype(h.dtype)
    out = jnp.concatenate([y_a, y_b], axis=-1) @ w_out
    return out, conv_ext[:, -(CONV_WIDTH - 1):], u_ext[:, -POOL_BUF:]


def diff_qkv(h, pos, w_qkv):
    b, t, _ = h.shape
    qkv = h @ w_qkv
    q = qkv[..., :D_MODEL].reshape(b, t, N_HEADS, 2, HEAD_DIM)
    k = qkv[..., D_MODEL:2 * D_MODEL].reshape(b, t, N_HEADS, 2, HEAD_DIM)
    v = qkv[..., 2 * D_MODEL:].reshape(b, t, N_HEADS, V_DIM)
    return rope(q, pos), rope(k, pos), v


def diff_lambda(lam_p, lambda_init):
    lp = lam_p.astype(jnp.float32)
    return jnp.exp(jnp.sum(lp[0] * lp[1])) - jnp.exp(jnp.sum(lp[2] * lp[3])) + lambda_init


def diff_softmax_attend(q, k, v, mask, lam):
    s = jnp.einsum('bqhcd,bkhcd->bhcqk', q, k, preferred_element_type=jnp.float32) * (HEAD_DIM ** -0.5)
    s = jnp.where(mask, s, NEG_INF)
    p = jax.nn.softmax(s, axis=-1)
    a = p[:, :, 0] - lam * p[:, :, 1]
    return jnp.einsum('bhqk,bkhe->bqhe', a, v.astype(jnp.float32)).astype(v.dtype)


def diff_out(o, subln_g, lambda_init, w_o):
    b, t = o.shape[:2]
    o = rms_norm(o, subln_g) * (1.0 - lambda_init)
    return o.reshape(b, t, N_HEADS * V_DIM) @ w_o


def prompt_diff_attention(q, k, v, lam):
    b, s = q.shape[:2]
    nb = s // Q_BLOCK
    qb = q.reshape(b, nb, Q_BLOCK, N_HEADS, 2, HEAD_DIM).transpose(1, 0, 2, 3, 4, 5)
    k_pos = jnp.arange(s)

    def block(args):
        q_blk, i = args
        q_pos = i * Q_BLOCK + jnp.arange(Q_BLOCK)
        mask = k_pos[None, :] <= q_pos[:, None]
        return diff_softmax_attend(q_blk, k, v, mask, lam)

    ob = lax.map(block, (qb, jnp.arange(nb)))
    return ob.transpose(1, 0, 2, 3, 4).reshape(b, s, N_HEADS, V_DIM)


def sample_diff_attention(q, k, v, cache_k, cache_v, layer_idx, page_table, lam):
    t = q.shape[1]
    past = page_table.shape[1] * PAGE_SIZE
    q_pos = past + jnp.arange(t)
    k_pos = jnp.arange(past + t)
    mask = k_pos[None, :] <= q_pos[:, None]

    def one(args):
        q1, k1, v1, pages = args
        kp = cache_k[layer_idx, pages].reshape(past, N_HEADS, 2, HEAD_DIM).astype(k1.dtype)
        vp = cache_v[layer_idx, pages].reshape(past, N_HEADS, V_DIM).astype(v1.dtype)
        kf = jnp.concatenate([kp, k1], axis=0)
        vf = jnp.concatenate([vp, v1], axis=0)
        return diff_softmax_attend(q1[None], kf[None], vf[None], mask, lam)[0]

    return lax.map(one, (q, k, v, page_table))


def setup_inputs(seed: int = 0) -> dict:
    key = jax.random.key(seed)
    ks = jax.random.split(key, 24)
    f32 = jnp.float32
    n_pages = PAST_LEN // PAGE_SIZE
    n_pool = (DEC_BATCH * n_pages * 5) // 4

    def nrm(k, shape, scale=1.0):
        return jax.random.normal(k, shape, f32) * scale

    perm = jax.random.permutation(ks[6], n_pool)
    page_table = perm[:DEC_BATCH * n_pages].reshape(DEC_BATCH, n_pages).astype(jnp.int32)
    return {
        'x_prompt': nrm(ks[0], (BATCH, SEQ, D_MODEL)),
        'x_sample': nrm(ks[1], (DEC_BATCH, DEC_SEQ, D_MODEL)),
        'state_conv': nrm(ks[2], (N_EVEN, DEC_BATCH, CONV_WIDTH - 1, D_CONV)),
        'state_pool': nrm(ks[3], (N_EVEN, DEC_BATCH, POOL_BUF, D_POOL)),
        'cache_k': nrm(ks[4], (N_ODD, n_pool, PAGE_SIZE, N_HEADS, 2, HEAD_DIM)),
        'cache_v': nrm(ks[5], (N_ODD, n_pool, PAGE_SIZE, N_HEADS, V_DIM)),
        'page_table': page_table,
        'norm_mix_pre': 1.0 + nrm(ks[7], (DEPTH, D_MODEL), 0.05),
        'norm_mix_post': 1.0 + nrm(ks[8], (DEPTH, D_MODEL), 0.05),
        'norm_mlp_pre': 1.0 + nrm(ks[9], (DEPTH, D_MODEL), 0.05),
        'norm_mlp_post': 1.0 + nrm(ks[10], (DEPTH, D_MODEL), 0.05),
        'mlp_up': nrm(ks[11], (DEPTH, D_MODEL, D_FF), D_MODEL ** -0.5),
        'mlp_down': nrm(ks[12], (DEPTH, D_FF, D_MODEL), D_FF ** -0.5),
        'sc_w_in': nrm(ks[13], (N_EVEN, D_MODEL, 3 * D_CONV + D_POOL), D_MODEL ** -0.5),
        'sc_conv_w': nrm(ks[14], (N_EVEN, CONV_WIDTH, D_CONV), CONV_WIDTH ** -0.5),
        'pool_w': nrm(ks[15], (N_EVEN, POOL_GROUPS, POOL_GC, POOL_GC), POOL_GC ** -0.5),
        'pool_scale': 1.0 + nrm(ks[16], (N_EVEN, D_POOL), 0.1),
        'sc_w_out': nrm(ks[17], (N_EVEN, D_CONV + D_POOL, D_MODEL), (D_CONV + D_POOL) ** -0.5),
        'attn_w_qkv': nrm(ks[18], (N_ODD, D_MODEL, 3 * D_MODEL), D_MODEL ** -0.5),
        'attn_lambda': nrm(ks[19], (N_ODD, 4, HEAD_DIM), 0.1),
        'attn_subln': 1.0 + nrm(ks[20], (N_ODD, V_DIM), 0.05),
        'attn_w_o': nrm(ks[21], (N_ODD, N_HEADS * V_DIM, D_MODEL), (N_HEADS * V_DIM) ** -0.5),
    }


def reference(x_prompt, x_sample, state_conv, state_pool, cache_k, cache_v, page_table,
              norm_mix_pre, norm_mix_post, norm_mlp_pre, norm_mlp_post, mlp_up, mlp_down,
              sc_w_in, sc_conv_w, pool_w, pool_scale, sc_w_out,
              attn_w_qkv, attn_lambda, attn_subln, attn_w_o):
    pos_p = jnp.arange(x_prompt.shape[1], dtype=jnp.int32)
    pos_s = PAST_LEN + jnp.arange(x_sample.shape[1], dtype=jnp.int32)
    yp, ys = x_prompt, x_sample
    conv_p, conv_s, pool_p, pool_s = [], [], [], []
    k_p, v_p, k_s, v_s = [], [], [], []
    for layer in range(DEPTH):
        hp = rms_norm(yp, norm_mix_pre[layer])
        hs = rms_norm(ys, norm_mix_pre[layer])
        j = layer // 2
        if layer % 2 == 0:
            zc = jnp.zeros((yp.shape[0], CONV_WIDTH - 1, D_CONV), yp.dtype)
            zp = jnp.zeros((yp.shape[0], POOL_BUF, D_POOL), yp.dtype)
            mp, c1, p1 = conv_pool_mixer(hp, zc, zp, pos_p, sc_w_in[j], sc_conv_w[j], pool_w[j], pool_scale[j], sc_w_out[j])
            ms, c2, p2 = conv_pool_mixer(hs, state_conv[j], state_pool[j], pos_s, sc_w_in[j], sc_conv_w[j], pool_w[j], pool_scale[j], sc_w_out[j])
            conv_p.append(c1)
            conv_s.append(c2)
            pool_p.append(p1)
            pool_s.append(p2)
        else:
            lam_init = 0.8 - 0.6 * math.exp(-0.3 * layer)
            lam = diff_lambda(attn_lambda[j], lam_init)
            qp, kp, vp = diff_qkv(hp, pos_p, attn_w_qkv[j])
            qs, ks_, vs_ = diff_qkv(hs, pos_s, attn_w_qkv[j])
            op = prompt_diff_attention(qp, kp, vp, lam)
            os_ = sample_diff_attention(qs, ks_, vs_, cache_k, cache_v, j, page_table, lam)
            mp = diff_out(op, attn_subln[j], lam_init, attn_w_o[j])
            ms = diff_out(os_, attn_subln[j], lam_init, attn_w_o[j])
            k_p.append(kp)
            v_p.append(vp)
            k_s.append(ks_)
            v_s.append(vs_)
        yp = yp + rms_norm(mp, norm_mix_post[layer])
        ys = ys + rms_norm(ms, norm_mix_post[layer])
        yp = yp + rms_norm(sq_relu_mlp(rms_norm(yp, norm_mlp_pre[layer]), mlp_up[layer], mlp_down[layer]), norm_mlp_post[layer])
        ys = ys + rms_norm(sq_relu_mlp(rms_norm(ys, norm_mlp_pre[layer]), mlp_up[layer], mlp_down[layer]), norm_mlp_post[layer])
    return (yp, ys, jnp.stack(conv_p), jnp.stack(conv_s), jnp.stack(pool_p), jnp.stack(pool_s),
            jnp.stack(k_p), jnp.stack(v_p), jnp.stack(k_s), jnp.stack(v_s))
```

```python
import functools
import math

import numpy as np
import jax
import jax.numpy as jnp
from jax import lax
from jax.experimental import pallas as pl
from jax.experimental.pallas import tpu as pltpu

D_MODEL = 1024
D_CONV = D_MODEL // 2
D_POOL = D_MODEL // 2
CONV_WIDTH = 3
POOL_WINDOWS = (2, 4, 8, 16)
POOL_GC = D_POOL // len(POOL_WINDOWS)
POOL_BUF = max(POOL_WINDOWS) - 1
HEAD_DIM = 64
N_HEADS = D_MODEL // (2 * HEAD_DIM)
V_DIM = 2 * HEAD_DIM
D_FF = 4 * D_MODEL
ROPE_THETA = 10000.0
NORM_EPS = 1e-6
PAGE_SIZE = 128
NEG_INF = -1e30
LOG2E = math.log2(math.e)

V7X_LANES = 128
V7X_SUBLANES = 8
V7X_VMEM_BYTES = 64 * 1024 * 1024
VMEM_LIMIT_BYTES = V7X_VMEM_BYTES - 8 * 1024 * 1024

CONV_HIST = V7X_SUBLANES
POOL_HIST = 2 * V7X_SUBLANES

F32 = jnp.float32
BF16 = jnp.bfloat16


def _resident(shape):
    nd = len(shape)
    return pl.BlockSpec(shape, lambda *_: (0,) * nd, pipeline_mode=pl.Buffered(1))


def _whole(shape):
    nd = len(shape)
    return pl.BlockSpec(shape, lambda *_: (0,) * nd)


def _params(*sem):
    return pltpu.CompilerParams(dimension_semantics=sem, vmem_limit_bytes=VMEM_LIMIT_BYTES)


def _rms(x, g):
    ms = jnp.mean(x * x, axis=-1, keepdims=True)
    return x * lax.rsqrt(ms + NORM_EPS) * g


def _dot(a, b):
    return jnp.dot(a, b, preferred_element_type=F32)


def _mlp_residual(y, g_pre, g_post, w_up_ref, w_down_ref):
    h = _rms(y, g_pre).astype(BF16)
    r = jnp.maximum(_dot(h, w_up_ref[...]), 0.0)
    out = _dot((r * r).astype(BF16), w_down_ref[...])
    return y + _rms(out, g_post)


def _conv_pool_outputs(proj, conv_prev, pool_prev, pos, conv_w, pool_w_ref, pool_scale, w_out_ref):
    b_gate = proj[:, :D_CONV]
    conv_in = proj[:, D_CONV:2 * D_CONV] * proj[:, 2 * D_CONV:3 * D_CONV]
    u = proj[:, 3 * D_CONV:]
    conv_out = conv_w[CONV_WIDTH - 1:CONV_WIDTH] * conv_in
    for k in range(1, CONV_WIDTH):
        conv_out = conv_out + conv_w[CONV_WIDTH - 1 - k:CONV_WIDTH - k] * conv_prev(k)
    y_a = b_gate * conv_out
    ybs = []
    for g, w in enumerate(POOL_WINDOWS):
        sl = slice(g * POOL_GC, (g + 1) * POOL_GC)
        ug = u[:, sl]
        s = ug
        for j in range(1, w):
            s = s + pool_prev(j, sl)
        if pos is None:
            d = s / float(w) - ug
        else:
            d = s / jnp.minimum(pos + 1, w).astype(F32) - ug
        ybs.append(_dot(d.astype(BF16), pool_w_ref[g]))
    y_b = jnp.concatenate(ybs, axis=-1) * pool_scale
    mix = jnp.concatenate([y_a, y_b], axis=-1).astype(BF16)
    return _dot(mix, w_out_ref[...]), conv_in, u


def _mixer0_prompt_kernel(x_ref, g_pre_ref, g_post_ref, w_in_ref, conv_w_ref, pool_w_ref, pool_scale_ref,
                          w_out_ref, gm_pre_ref, gm_post_ref, w_up_ref, w_down_ref,
                          y_ref, conv_state_ref, pool_state_ref, conv_ext, pool_ext, *, tm):
    t = pl.program_id(1)

    @pl.when(t == 0)
    def _():
        conv_ext[0:CONV_HIST, :] = jnp.zeros((CONV_HIST, D_CONV), F32)
        pool_ext[0:POOL_HIST, :] = jnp.zeros((POOL_HIST, D_POOL), F32)

    x = x_ref[0]
    h = _rms(x, g_pre_ref[...]).astype(BF16)
    proj = _dot(h, w_in_ref[...])
    conv_ext[CONV_HIST:CONV_HIST + tm, :] = proj[:, D_CONV:2 * D_CONV] * proj[:, 2 * D_CONV:3 * D_CONV]
    pool_ext[POOL_HIST:POOL_HIST + tm, :] = proj[:, 3 * D_CONV:]
    pos = t * tm + lax.broadcasted_iota(jnp.int32, (tm, 1), 0)
    out, _, _ = _conv_pool_outputs(
        proj,
        lambda k: conv_ext[CONV_HIST - k:CONV_HIST - k + tm, :],
        lambda j, sl: pool_ext[POOL_HIST - j:POOL_HIST - j + tm, sl],
        pos, conv_w_ref[...], pool_w_ref, pool_scale_ref[...], w_out_ref)
    y1 = x + _rms(out, g_post_ref[...])
    y_ref[0] = _mlp_residual(y1, gm_pre_ref[...], gm_post_ref[...], w_up_ref, w_down_ref)

    @pl.when(t == pl.num_programs(1) - 1)
    def _():
        conv_state_ref[0] = conv_ext[CONV_HIST + tm - (CONV_WIDTH - 1):CONV_HIST + tm, :]
        pool_state_ref[0] = pool_ext[POOL_HIST + tm - POOL_BUF:POOL_HIST + tm, :]

    conv_ext[0:CONV_HIST, :] = conv_ext[tm:tm + CONV_HIST, :]
    pool_ext[0:POOL_HIST, :] = pool_ext[tm:tm + POOL_HIST, :]


def _mixer0_prompt(x, g_pre, g_post, w_in, conv_w, pool_w, pool_scale, w_out, gm_pre, gm_post, w_up, w_down):
    b, s, _ = x.shape
    tm = min(512, s)
    assert s % tm == 0 and tm >= POOL_HIST
    row = pl.BlockSpec((1, tm, D_MODEL), lambda i, t: (i, t, 0))
    return pl.pallas_call(
        functools.partial(_mixer0_prompt_kernel, tm=tm),
        grid=(b, s // tm),
        in_specs=[row, _resident(g_pre.shape), _resident(g_post.shape), _resident(w_in.shape),
                  _resident(conv_w.shape), _resident(pool_w.shape), _resident(pool_scale.shape),
                  _resident(w_out.shape), _resident(gm_pre.shape), _resident(gm_post.shape),
                  _resident(w_up.shape), _resident(w_down.shape)],
        out_specs=[row,
                   pl.BlockSpec((1, CONV_WIDTH - 1, D_CONV), lambda i, t: (i, 0, 0)),
                   pl.BlockSpec((1, POOL_BUF, D_POOL), lambda i, t: (i, 0, 0))],
        out_shape=[jax.ShapeDtypeStruct((b, s, D_MODEL), F32),
                   jax.ShapeDtypeStruct((b, CONV_WIDTH - 1, D_CONV), F32),
                   jax.ShapeDtypeStruct((b, POOL_BUF, D_POOL), F32)],
        scratch_shapes=[pltpu.VMEM((CONV_HIST + tm, D_CONV), F32), pltpu.VMEM((POOL_HIST + tm, D_POOL), F32)],
        compiler_params=_params("arbitrary", "arbitrary"),
        name="mixer0_prompt",
    )(x, g_pre, g_post, w_in, conv_w, pool_w, pool_scale, w_out, gm_pre, gm_post, w_up, w_down)


def _mixer0_sample_kernel(x_ref, conv_hist_ref, pool_hist_ref, g_pre_ref, g_post_ref, w_in_ref, conv_w_ref,
                          pool_w_ref, pool_scale_ref, w_out_ref, gm_pre_ref, gm_post_ref, w_up_ref, w_down_ref,
                          y_ref, conv_state_ref, pool_state_ref, *, full_windows):
    x = x_ref[...]
    h = _rms(x, g_pre_ref[...]).astype(BF16)
    proj = _dot(h, w_in_ref[...])
    assert full_windows
    out, conv_in, u = _conv_pool_outputs(
        proj,
        lambda k: conv_hist_ref[CONV_WIDTH - 1 - k],
        lambda j, sl: pool_hist_ref[POOL_BUF - j][:, sl],
        None, conv_w_ref[...], pool_w_ref, pool_scale_ref[...], w_out_ref)
    y1 = x + _rms(out, g_post_ref[...])
    y_ref[...] = _mlp_residual(y1, gm_pre_ref[...], gm_post_ref[...], w_up_ref, w_down_ref)
    for k in range(CONV_WIDTH - 2):
        conv_state_ref[k] = conv_hist_ref[k + 1]
    conv_state_ref[CONV_WIDTH - 2] = conv_in
    for j in range(POOL_BUF - 1):
        pool_state_ref[j] = pool_hist_ref[j + 1]
    pool_state_ref[POOL_BUF - 1] = u


def _mixer0_sample(x, conv_hist, pool_hist, past_len, *weights):
    n = x.shape[0]
    args = (x, conv_hist, pool_hist) + weights
    return pl.pallas_call(
        functools.partial(_mixer0_sample_kernel, full_windows=past_len + 1 >= max(POOL_WINDOWS)),
        grid=(1,),
        in_specs=[_resident(a.shape) for a in args],
        out_specs=[_whole((n, D_MODEL)), _whole(conv_hist.shape), _whole(pool_hist.shape)],
        out_shape=[jax.ShapeDtypeStruct((n, D_MODEL), F32),
                   jax.ShapeDtypeStruct(conv_hist.shape, F32),
                   jax.ShapeDtypeStruct(pool_hist.shape, F32)],
        compiler_params=_params("arbitrary"),
        name="mixer0_sample",
    )(*args)


def _rope(x, cos, sin_signed, first_half):
    outs = []
    for j in range(D_MODEL // V7X_LANES):
        xj = x[:, j * V7X_LANES:(j + 1) * V7X_LANES]
        partner = jnp.where(first_half,
                            pltpu.roll(xj, V7X_LANES - HEAD_DIM // 2, 1),
                            pltpu.roll(xj, HEAD_DIM // 2, 1))
        outs.append(xj * cos + partner * sin_signed)
    return jnp.concatenate(outs, axis=-1)


def _qkv_kernel(y_ref, g_ref, w_ref, inv_freq_ref, q_ref, k_ref, v_ref, kb_ref, vb_ref, cos_tab, sin_tab,
                *, tm, tiles_per_seq, const_pos):
    i = pl.program_id(0)
    inv_freq = inv_freq_ref[...]
    lane = lax.broadcasted_iota(jnp.int32, (1, V7X_LANES), 1)
    first_half = (lane % HEAD_DIM) < HEAD_DIM // 2
    if const_pos is None:
        @pl.when(i == 0)
        def _():
            r = lax.broadcasted_iota(jnp.int32, (tm, V7X_LANES), 0).astype(F32)
            cos_tab[...] = jnp.cos(r * inv_freq)
            sin_tab[...] = jnp.sin(r * inv_freq)

        base = ((i % tiles_per_seq) * tm).astype(F32) * inv_freq
        cb, sb = jnp.cos(base), jnp.sin(base)
        cos = cb * cos_tab[...] - sb * sin_tab[...]
        sin = sb * cos_tab[...] + cb * sin_tab[...]
    else:
        ang = float(const_pos) * inv_freq
        cos, sin = jnp.cos(ang), jnp.sin(ang)
    sin_signed = jnp.where(first_half, -sin, sin)

    h = _rms(y_ref[...], g_ref[...]).astype(BF16)
    qkv = _dot(h, w_ref[...])
    q = _rope(qkv[:, :D_MODEL], cos, sin_signed, first_half)
    k = _rope(qkv[:, D_MODEL:2 * D_MODEL], cos, sin_signed, first_half)
    v = qkv[:, 2 * D_MODEL:]
    q_ref[...] = (q * (HEAD_DIM ** -0.5 * LOG2E)).astype(BF16)
    k_ref[...] = k
    v_ref[...] = v
    kb_ref[...] = k.astype(BF16)
    vb_ref[...] = v.astype(BF16)


def _inv_freq_lanes():
    half = HEAD_DIM // 2
    inv = 1.0 / (ROPE_THETA ** (np.arange(half, dtype=np.float32) / np.float32(half)))
    return jnp.asarray(np.tile(inv.astype(np.float32), V7X_LANES // half)[None, :])


def _qkv(y2d, g, w_qkv, seq_len, const_pos):
    m = y2d.shape[0]
    tm = min(512, seq_len) if const_pos is None else m
    assert m % tm == 0 and seq_len % tm == 0
    row = pl.BlockSpec((tm, D_MODEL), lambda i: (i, 0))
    inv_freq = _inv_freq_lanes()
    return pl.pallas_call(
        functools.partial(_qkv_kernel, tm=tm, tiles_per_seq=seq_len // tm, const_pos=const_pos),
        grid=(m // tm,),
        in_specs=[row, _resident(g.shape), _resident(w_qkv.shape), _resident(inv_freq.shape)],
        out_specs=[row] * 5,
        out_shape=[jax.ShapeDtypeStruct((m, D_MODEL), BF16),
                   jax.ShapeDtypeStruct((m, D_MODEL), F32),
                   jax.ShapeDtypeStruct((m, D_MODEL), F32),
                   jax.ShapeDtypeStruct((m, D_MODEL), BF16),
                   jax.ShapeDtypeStruct((m, D_MODEL), BF16)],
        scratch_shapes=[pltpu.VMEM((tm, V7X_LANES), F32), pltpu.VMEM((tm, V7X_LANES), F32)],
        compiler_params=_params("arbitrary"),
        name="qkv_rope",
    )(y2d, g, w_qkv, inv_freq)


def _diff_lambda(lam_p, lam_init):
    a = jnp.sum(lam_p[0:1] * lam_p[1:2], axis=-1, keepdims=True)
    b = jnp.sum(lam_p[2:3] * lam_p[3:4], axis=-1, keepdims=True)
    return jnp.exp(a) - jnp.exp(b) + lam_init


def _prompt_attn_kernel(q_ref, k_ref, v_ref, lam_ref, g_ref, o_ref, *, tq, lam_init):
    qi = pl.program_id(2)
    q = q_ref[0]
    qc = (q[:, :HEAD_DIM], q[:, HEAD_DIM:])

    def kv_step(j, carry, masked):
        kblk = k_ref[0, pl.ds(pl.multiple_of(j * tq, tq), tq), :]
        vblk = v_ref[0, pl.ds(pl.multiple_of(j * tq, tq), tq), :]
        new = []
        for c in range(2):
            m, l, acc = carry[c]
            s = lax.dot_general(qc[c], kblk[:, c * HEAD_DIM:(c + 1) * HEAD_DIM],
                                (((1,), (1,)), ((), ())), preferred_element_type=F32)
            if masked:
                row = lax.broadcasted_iota(jnp.int32, (tq, tq), 0)
                col = lax.broadcasted_iota(jnp.int32, (tq, tq), 1)
                s = jnp.where(col <= row, s, NEG_INF)
            m_new = jnp.maximum(m, jnp.max(s, axis=-1, keepdims=True))
            alpha = jnp.exp2(m - m_new)
            p = jnp.exp2(s - m_new)
            l = alpha * l + jnp.sum(p, axis=-1, keepdims=True)
            acc = alpha * acc + _dot(p.astype(BF16), vblk)
            new.append((m_new, l, acc))
        return tuple(new)

    init = tuple((jnp.full((tq, 1), NEG_INF, F32), jnp.zeros((tq, 1), F32), jnp.zeros((tq, V_DIM), F32))
                 for _ in range(2))
    carry = lax.fori_loop(0, qi, lambda j, c: kv_step(j, c, False), init)
    (_, l1, a1), (_, l2, a2) = kv_step(qi, carry, True)
    lam = _diff_lambda(lam_ref[...], lam_init)
    o = a1 / l1 - lam * (a2 / l2)
    o_ref[0] = (_rms(o, g_ref[...]) * (1.0 - lam_init)).astype(BF16)


def _prompt_attention(q, kb, vb, lam_p, subln_g, lam_init):
    b, s, _ = q.shape
    tq = min(512, s)
    assert s % tq == 0
    return pl.pallas_call(
        functools.partial(_prompt_attn_kernel, tq=tq, lam_init=lam_init),
        grid=(b, N_HEADS, s // tq),
        in_specs=[pl.BlockSpec((1, tq, V_DIM), lambda i, h, t: (i, t, h)),
                  pl.BlockSpec((1, s, V_DIM), lambda i, h, t: (i, 0, h)),
                  pl.BlockSpec((1, s, V_DIM), lambda i, h, t: (i, 0, h)),
                  pl.BlockSpec(lam_p.shape, lambda i, h, t: (0, 0)),
                  pl.BlockSpec(subln_g.shape, lambda i, h, t: (0, 0))],
        out_specs=pl.BlockSpec((1, tq, V_DIM), lambda i, h, t: (i, t, h)),
        out_shape=jax.ShapeDtypeStruct((b, s, D_MODEL), BF16),
        compiler_params=_params("arbitrary", "arbitrary", "arbitrary"),
        name="prompt_diff_attention",
    )(q, kb, vb, lam_p, subln_g)


def _decode_attn_kernel(pt_ref, q_ref, ks_ref, vs_ref, kc_ref, vc_ref, lam_ref, g_ref, o_ref,
                        qblk, m_s, l_s, acc_s, *, lam_init):
    p = pl.program_id(1)
    n_maps = 2 * N_HEADS
    rowi = lax.broadcasted_iota(jnp.int32, (n_maps, D_MODEL), 0)
    coli = lax.broadcasted_iota(jnp.int32, (n_maps, D_MODEL), 1)
    own_k = (coli // HEAD_DIM) == ((rowi % N_HEADS) * 2 + rowi // N_HEADS)

    @pl.when(p == 0)
    def _():
        qrow = q_ref[0].astype(F32)
        qb = jnp.where(own_k, jnp.broadcast_to(qrow, (n_maps, D_MODEL)), 0.0)
        qblk[...] = qb.astype(BF16)
        m_s[...] = jnp.sum(qb * ks_ref[0], axis=-1, keepdims=True)
        l_s[...] = jnp.ones((n_maps, 1), F32)
        acc_s[...] = jnp.broadcast_to(vs_ref[0], (n_maps, D_MODEL))

    s = lax.dot_general(qblk[...], kc_ref[0].astype(BF16), (((1,), (1,)), ((), ())),
                        preferred_element_type=F32)
    m_new = jnp.maximum(m_s[...], jnp.max(s, axis=-1, keepdims=True))
    alpha = jnp.exp2(m_s[...] - m_new)
    pr = jnp.exp2(s - m_new)
    l_s[...] = alpha * l_s[...] + jnp.sum(pr, axis=-1, keepdims=True)
    acc_s[...] = alpha * acc_s[...] + _dot(pr.astype(BF16), vc_ref[0].astype(BF16))
    m_s[...] = m_new

    @pl.when(p == pl.num_programs(1) - 1)
    def _():
        o_all = acc_s[...] / l_s[...]
        lam = _diff_lambda(lam_ref[...], lam_init)
        diff = o_all[:N_HEADS] - lam * o_all[N_HEADS:]
        own_v = (lax.broadcasted_iota(jnp.int32, (N_HEADS, D_MODEL), 1) // V_DIM
                 == lax.broadcasted_iota(jnp.int32, (N_HEADS, D_MODEL), 0))
        diff = jnp.where(own_v, diff, 0.0)
        ms = jnp.sum(diff * diff, axis=-1, keepdims=True) * (1.0 / V_DIM)
        normed = diff * lax.rsqrt(ms + NORM_EPS)
        o = jnp.sum(normed, axis=0, keepdims=True) * g_ref[...] * (1.0 - lam_init)
        o_ref[0] = o.astype(BF16)


def _decode_attention(q, k_self, v_self, cache_k, cache_v, page_table, lam_p, subln_row, lam_init):
    n, n_pages = page_table.shape
    page = pl.BlockSpec((1, PAGE_SIZE, D_MODEL), lambda i, p, pt: (pt[i * n_pages + p], 0, 0))
    one = pl.BlockSpec((1, 1, D_MODEL), lambda i, p, pt: (i, 0, 0))
    n_maps = 2 * N_HEADS
    return pl.pallas_call(
        functools.partial(_decode_attn_kernel, lam_init=lam_init),
        grid_spec=pltpu.PrefetchScalarGridSpec(
            num_scalar_prefetch=1,
            grid=(n, n_pages),
            in_specs=[one, one, one, page, page,
                      pl.BlockSpec(lam_p.shape, lambda i, p, pt: (0, 0)),
                      pl.BlockSpec(subln_row.shape, lambda i, p, pt: (0, 0))],
            out_specs=one,
            scratch_shapes=[pltpu.VMEM((n_maps, D_MODEL), BF16), pltpu.VMEM((n_maps, 1), F32),
                            pltpu.VMEM((n_maps, 1), F32), pltpu.VMEM((n_maps, D_MODEL), F32)]),
        out_shape=jax.ShapeDtypeStruct((n, 1, D_MODEL), BF16),
        compiler_params=_params("arbitrary", "arbitrary"),
        name="decode_diff_attention",
    )(page_table.reshape(-1), q, k_self, v_self, cache_k, cache_v, lam_p, subln_row)


def _attn_out_kernel(y_ref, o_ref, w_o_ref, g_post_ref, gm_pre_ref, gm_post_ref, w_up_ref, w_down_ref, out_ref):
    y1 = y_ref[...] + _rms(_dot(o_ref[...], w_o_ref[...]), g_post_ref[...])
    out_ref[...] = _mlp_residual(y1, gm_pre_ref[...], gm_post_ref[...], w_up_ref, w_down_ref)


def _attn_out(y2d, o2d, w_o, g_post, gm_pre, gm_post, w_up, w_down):
    m = y2d.shape[0]
    tm = min(512, m)
    assert m % tm == 0
    row = pl.BlockSpec((tm, D_MODEL), lambda i: (i, 0))
    weights = (w_o, g_post, gm_pre, gm_post, w_up, w_down)
    return pl.pallas_call(
        _attn_out_kernel,
        grid=(m // tm,),
        in_specs=[row, row] + [_resident(a.shape) for a in weights],
        out_specs=row,
        out_shape=jax.ShapeDtypeStruct((m, D_MODEL), F32),
        compiler_params=_params("arbitrary"),
        name="attn_out_mlp",
    )(y2d, o2d, *weights)


def kernel(x_prompt, x_sample, state_conv, state_pool, cache_k, cache_v, page_table, norm_mix_pre, norm_mix_post, norm_mlp_pre, norm_mlp_post, mlp_up, mlp_down, sc_w_in, sc_conv_w, pool_w, pool_scale, sc_w_out, attn_w_qkv, attn_lambda, attn_subln, attn_w_o):
    b, s, _ = x_prompt.shape
    n, t_dec, _ = x_sample.shape
    assert t_dec == 1 and norm_mix_pre.shape[0] == 2
    past_len = page_table.shape[1] * PAGE_SIZE
    lam_init = 0.8 - 0.6 * math.exp(-0.3 * 1)

    def gain(a, layer):
        return a[layer].reshape(1, D_MODEL)

    bf = lambda a: a.astype(BF16)
    l0 = (gain(norm_mix_pre, 0), gain(norm_mix_post, 0), bf(sc_w_in[0]), sc_conv_w[0], bf(pool_w[0]),
          pool_scale[0].reshape(1, D_POOL), bf(sc_w_out[0]), gain(norm_mlp_pre, 0), gain(norm_mlp_post, 0),
          bf(mlp_up[0]), bf(mlp_down[0]))
    l1_out = (bf(attn_w_o[0]), gain(norm_mix_post, 1), gain(norm_mlp_pre, 1), gain(norm_mlp_post, 1),
              bf(mlp_up[1]), bf(mlp_down[1]))
    w_qkv = bf(attn_w_qkv[0])
    g_qkv = gain(norm_mix_pre, 1)
    lam_p = attn_lambda[0]
    subln = attn_subln[0].reshape(1, V_DIM)

    yp, conv_p, pool_p = _mixer0_prompt(x_prompt, *l0)
    ys, conv_s, pool_s = _mixer0_sample(x_sample.reshape(n, D_MODEL),
                                        jnp.swapaxes(state_conv[0], 0, 1), jnp.swapaxes(state_pool[0], 0, 1),
                                        past_len, *l0)

    qp, kp, vp, kbp, vbp = _qkv(yp.reshape(b * s, D_MODEL), g_qkv, w_qkv, s, None)
    qs, ks, vs, _, _ = _qkv(ys, g_qkv, w_qkv, n, past_len)
    op = _prompt_attention(qp.reshape(b, s, D_MODEL), kbp.reshape(b, s, D_MODEL), vbp.reshape(b, s, D_MODEL),
                           lam_p, subln, lam_init)
    os_ = _decode_attention(qs.reshape(n, 1, D_MODEL), ks.reshape(n, 1, D_MODEL), vs.reshape(n, 1, D_MODEL),
                            cache_k[0].reshape(-1, PAGE_SIZE, D_MODEL), cache_v[0].reshape(-1, PAGE_SIZE, D_MODEL),
                            page_table, lam_p, jnp.tile(subln, (1, N_HEADS)), lam_init)
    yp = _attn_out(yp.reshape(b * s, D_MODEL), op.reshape(b * s, D_MODEL), *l1_out)
    ys = _attn_out(ys, os_.reshape(n, D_MODEL), *l1_out)

    return (yp.reshape(b, s, D_MODEL), ys.reshape(n, 1, D_MODEL),
            conv_p[None], jnp.swapaxes(conv_s, 0, 1)[None],
            pool_p[None], jnp.swapaxes(pool_s, 0, 1)[None],
            kp.reshape(1, b, s, N_HEADS, 2, HEAD_DIM), vp.reshape(1, b, s, N_HEADS, V_DIM),
            ks.reshape(1, n, 1, N_HEADS, 2, HEAD_DIM), vs.reshape(1, n, 1, N_HEADS, V_DIM))
```

```python
import functools
import math

import jax
import jax.numpy as jnp
from jax import lax
from jax.experimental import pallas as pl
from jax.experimental.pallas import tpu as pltpu

D_MODEL = 1024
D_CONV = D_MODEL // 2
D_POOL = D_MODEL // 2
CONV_WIDTH = 3
POOL_WINDOWS = (2, 4, 8, 16)
POOL_GC = D_POOL // len(POOL_WINDOWS)
POOL_BUF = max(POOL_WINDOWS) - 1
HEAD_DIM = 64
N_HEADS = D_MODEL // (2 * HEAD_DIM)
N_MAPS = 2 * N_HEADS
V_DIM = 2 * HEAD_DIM
D_FF = 4 * D_MODEL
ROPE_THETA = 10000.0
NORM_EPS = 1e-6
PAGE_SIZE = 128
NEG_INF = -1e30
LOG2E = math.log2(math.e)

V7X_LANES = 128
V7X_SUBLANES = 8
V7X_VMEM_BYTES = 64 * 1024 * 1024
VMEM_LIMIT_BYTES = V7X_VMEM_BYTES - 8 * 1024 * 1024

CONV_HIST = V7X_SUBLANES
POOL_HIST = 2 * V7X_SUBLANES
ROW_TILE = 512
KV_TILE = 2048
PAGES_PER_STEP = 8

F32 = jnp.float32
BF16 = jnp.bfloat16


def _resident(shape):
    nd = len(shape)
    return pl.BlockSpec(shape, lambda *_: (0,) * nd, pipeline_mode=pl.Buffered(1))


def _whole(shape):
    nd = len(shape)
    return pl.BlockSpec(shape, lambda *_: (0,) * nd)


def _params(*sem):
    return pltpu.CompilerParams(dimension_semantics=sem, vmem_limit_bytes=VMEM_LIMIT_BYTES)


def _rms(x, g):
    ms = jnp.mean(x * x, axis=-1, keepdims=True)
    return x * lax.rsqrt(ms + NORM_EPS) * g


def _dot(a, b):
    return jnp.dot(a, b, preferred_element_type=F32)


def _dot_nt(a, b):
    return lax.dot_general(a, b, (((1,), (1,)), ((), ())), preferred_element_type=F32)


def _mlp_residual(y, g_pre, g_post, w_up_ref, w_down_ref):
    h = _rms(y, g_pre).astype(BF16)
    r = jnp.maximum(_dot(h, w_up_ref[...]), 0.0)
    out = _dot((r * r).astype(BF16), w_down_ref[...])
    return y + _rms(out, g_post)


def _conv_pool_outputs(proj, conv_prev, pool_prev, pos, conv_w, pool_w_ref, pool_scale, w_out_ref):
    b_gate = proj[:, :D_CONV]
    conv_in = proj[:, D_CONV:2 * D_CONV] * proj[:, 2 * D_CONV:3 * D_CONV]
    u = proj[:, 3 * D_CONV:]
    conv_out = conv_w[CONV_WIDTH - 1:CONV_WIDTH] * conv_in
    for k in range(1, CONV_WIDTH):
        conv_out = conv_out + conv_w[CONV_WIDTH - 1 - k:CONV_WIDTH - k] * conv_prev(k)
    y_a = b_gate * conv_out
    ybs = []
    for g, w in enumerate(POOL_WINDOWS):
        sl = slice(g * POOL_GC, (g + 1) * POOL_GC)
        ug = u[:, sl]
        s = ug
        for j in range(1, w):
            s = s + pool_prev(j, sl)
        if pos is None:
            d = s / float(w) - ug
        else:
            d = s / jnp.minimum(pos + 1, w).astype(F32) - ug
        ybs.append(_dot(d.astype(BF16), pool_w_ref[g]))
    y_b = jnp.concatenate(ybs, axis=-1) * pool_scale
    mix = jnp.concatenate([y_a, y_b], axis=-1).astype(BF16)
    return _dot(mix, w_out_ref[...]), conv_in, u


def _mixer0_prompt_kernel(x_ref, g_pre_ref, g_post_ref, w_in_ref, conv_w_ref, pool_w_ref, pool_scale_ref,
                          w_out_ref, gm_pre_ref, gm_post_ref, w_up_ref, w_down_ref,
                          y_ref, conv_state_ref, pool_state_ref, conv_ext, pool_ext, *, tm):
    t = pl.program_id(1)

    @pl.when(t == 0)
    def _():
        conv_ext[0:CONV_HIST, :] = jnp.zeros((CONV_HIST, D_CONV), F32)
        pool_ext[0:POOL_HIST, :] = jnp.zeros((POOL_HIST, D_POOL), F32)

    x = x_ref[0]
    h = _rms(x, g_pre_ref[...]).astype(BF16)
    proj = _dot(h, w_in_ref[...])
    conv_ext[CONV_HIST:CONV_HIST + tm, :] = proj[:, D_CONV:2 * D_CONV] * proj[:, 2 * D_CONV:3 * D_CONV]
    pool_ext[POOL_HIST:POOL_HIST + tm, :] = proj[:, 3 * D_CONV:]
    pos = t * tm + lax.broadcasted_iota(jnp.int32, (tm, 1), 0)
    out, _, _ = _conv_pool_outputs(
        proj,
        lambda k: conv_ext[CONV_HIST - k:CONV_HIST - k + tm, :],
        lambda j, sl: pool_ext[POOL_HIST - j:POOL_HIST - j + tm, sl],
        pos, conv_w_ref[...], pool_w_ref, pool_scale_ref[...], w_out_ref)
    y1 = x + _rms(out, g_post_ref[...])
    y_ref[0] = _mlp_residual(y1, gm_pre_ref[...], gm_post_ref[...], w_up_ref, w_down_ref)

    @pl.when(t == pl.num_programs(1) - 1)
    def _():
        conv_state_ref[0] = conv_ext[CONV_HIST + tm - (CONV_WIDTH - 1):CONV_HIST + tm, :]
        pool_state_ref[0] = pool_ext[POOL_HIST + tm - POOL_BUF:POOL_HIST + tm, :]

    conv_ext[0:CONV_HIST, :] = conv_ext[tm:tm + CONV_HIST, :]
    pool_ext[0:POOL_HIST, :] = pool_ext[tm:tm + POOL_HIST, :]


def _mixer0_prompt(x, g_pre, g_post, w_in, conv_w, pool_w, pool_scale, w_out, gm_pre, gm_post, w_up, w_down):
    b, s, _ = x.shape
    tm = min(ROW_TILE, s)
    assert s % tm == 0 and tm >= POOL_HIST
    row = pl.BlockSpec((1, tm, D_MODEL), lambda i, t: (i, t, 0))
    return pl.pallas_call(
        functools.partial(_mixer0_prompt_kernel, tm=tm),
        grid=(b, s // tm),
        in_specs=[row, _resident(g_pre.shape), _resident(g_post.shape), _resident(w_in.shape),
                  _resident(conv_w.shape), _resident(pool_w.shape), _resident(pool_scale.shape),
                  _resident(w_out.shape), _resident(gm_pre.shape), _resident(gm_post.shape),
                  _resident(w_up.shape), _resident(w_down.shape)],
        out_specs=[row,
                   pl.BlockSpec((1, CONV_WIDTH - 1, D_CONV), lambda i, t: (i, 0, 0)),
                   pl.BlockSpec((1, POOL_BUF, D_POOL), lambda i, t: (i, 0, 0))],
        out_shape=[jax.ShapeDtypeStruct((b, s, D_MODEL), F32),
                   jax.ShapeDtypeStruct((b, CONV_WIDTH - 1, D_CONV), F32),
                   jax.ShapeDtypeStruct((b, POOL_BUF, D_POOL), F32)],
        scratch_shapes=[pltpu.VMEM((CONV_HIST + tm, D_CONV), F32), pltpu.VMEM((POOL_HIST + tm, D_POOL), F32)],
        compiler_params=_params("arbitrary", "arbitrary"),
        name="mixer0_prompt",
    )(x, g_pre, g_post, w_in, conv_w, pool_w, pool_scale, w_out, gm_pre, gm_post, w_up, w_down)


def _mixer0_sample_kernel(x_ref, conv_hist_ref, pool_hist_ref, g_pre_ref, g_post_ref, w_in_ref, conv_w_ref,
                          pool_w_ref, pool_scale_ref, w_out_ref, gm_pre_ref, gm_post_ref, w_up_ref, w_down_ref,
                          y_ref, conv_state_ref, pool_state_ref, *, full_windows):
    x = x_ref[...]
    h = _rms(x, g_pre_ref[...]).astype(BF16)
    proj = _dot(h, w_in_ref[...])
    assert full_windows
    out, conv_in, u = _conv_pool_outputs(
        proj,
        lambda k: conv_hist_ref[CONV_WIDTH - 1 - k],
        lambda j, sl: pool_hist_ref[POOL_BUF - j][:, sl],
        None, conv_w_ref[...], pool_w_ref, pool_scale_ref[...], w_out_ref)
    y1 = x + _rms(out, g_post_ref[...])
    y_ref[...] = _mlp_residual(y1, gm_pre_ref[...], gm_post_ref[...], w_up_ref, w_down_ref)
    for k in range(CONV_WIDTH - 2):
        conv_state_ref[k] = conv_hist_ref[k + 1]
    conv_state_ref[CONV_WIDTH - 2] = conv_in
    for j in range(POOL_BUF - 1):
        pool_state_ref[j] = pool_hist_ref[j + 1]
    pool_state_ref[POOL_BUF - 1] = u


def _mixer0_sample(x, conv_hist, pool_hist, past_len, *weights):
    n = x.shape[0]
    args = (x, conv_hist, pool_hist) + weights
    return pl.pallas_call(
        functools.partial(_mixer0_sample_kernel, full_windows=past_len + 1 >= max(POOL_WINDOWS)),
        grid=(1,),
        in_specs=[_resident(a.shape) for a in args],
        out_specs=[_whole((n, D_MODEL)), _whole(conv_hist.shape), _whole(pool_hist.shape)],
        out_shape=[jax.ShapeDtypeStruct((n, D_MODEL), F32),
                   jax.ShapeDtypeStruct(conv_hist.shape, F32),
                   jax.ShapeDtypeStruct(pool_hist.shape, F32)],
        compiler_params=_params("arbitrary"),
        name="mixer0_sample",
    )(*args)


def _inv_freq(idx):
    half = HEAD_DIM // 2
    return jnp.exp((idx % half).astype(F32) * (-math.log(ROPE_THETA) / half))


def _rope_lanes(x, cos, sin):
    lane = lax.broadcasted_iota(jnp.int32, (1, V7X_LANES), 1)
    first_half = (lane % HEAD_DIM) < HEAD_DIM // 2
    sin_signed = jnp.where(first_half, -sin, sin)
    outs = []
    for j in range(D_MODEL // V7X_LANES):
        xj = x[:, j * V7X_LANES:(j + 1) * V7X_LANES]
        partner = jnp.where(first_half,
                            pltpu.roll(xj, V7X_LANES - HEAD_DIM // 2, 1),
                            pltpu.roll(xj, HEAD_DIM // 2, 1))
        outs.append(xj * cos + partner * sin_signed)
    return jnp.concatenate(outs, axis=-1)


def _rope_rows(xt, cos, sin):
    half = HEAD_DIM // 2
    outs = []
    for g in range(D_MODEL // HEAD_DIM):
        x1 = xt[g * HEAD_DIM:g * HEAD_DIM + half]
        x2 = xt[g * HEAD_DIM + half:(g + 1) * HEAD_DIM]
        outs += [x1 * cos - x2 * sin, x2 * cos + x1 * sin]
    return jnp.concatenate(outs, axis=0)


def _qkv_prompt_kernel(y_ref, g_ref, wq_ref, wkt_ref, wv_ref, q_ref, kt_ref, v_ref, ktb_ref, vb_ref,
                       cos_l, sin_l, cos_t, sin_t, *, tm):
    t = pl.program_id(1)
    half = HEAD_DIM // 2
    f_l = _inv_freq(lax.broadcasted_iota(jnp.int32, (1, V7X_LANES), 1))
    f_t = _inv_freq(lax.broadcasted_iota(jnp.int32, (half, 1), 0))

    @pl.when((pl.program_id(0) == 0) & (t == 0))
    def _():
        r_l = lax.broadcasted_iota(jnp.int32, (tm, V7X_LANES), 0).astype(F32)
        cos_l[...] = jnp.cos(r_l * f_l)
        sin_l[...] = jnp.sin(r_l * f_l)
        r_t = lax.broadcasted_iota(jnp.int32, (half, tm), 1).astype(F32)
        cos_t[...] = jnp.cos(r_t * f_t)
        sin_t[...] = jnp.sin(r_t * f_t)

    t0 = (t * tm).astype(F32)
    cb, sb = jnp.cos(t0 * f_l), jnp.sin(t0 * f_l)
    cos_q = cb * cos_l[...] - sb * sin_l[...]
    sin_q = sb * cos_l[...] + cb * sin_l[...]
    cb, sb = jnp.cos(t0 * f_t), jnp.sin(t0 * f_t)
    cos_k = cb * cos_t[...] - sb * sin_t[...]
    sin_k = sb * cos_t[...] + cb * sin_t[...]

    h = _rms(y_ref[0], g_ref[...]).astype(BF16)
    q = _rope_lanes(_dot(h, wq_ref[...]), cos_q, sin_q)
    q_ref[0] = (q * (HEAD_DIM ** -0.5 * LOG2E)).astype(BF16)
    kt = _rope_rows(_dot_nt(wkt_ref[...], h), cos_k, sin_k)
    kt_ref[0] = kt
    ktb_ref[0] = kt.astype(BF16)
    v = _dot(h, wv_ref[...])
    v_ref[0] = v
    vb_ref[0] = v.astype(BF16)


def _qkv_prompt(y, g, w_q, w_kt, w_v):
    b, s, _ = y.shape
    tm = min(ROW_TILE, s)
    assert s % tm == 0
    row = pl.BlockSpec((1, tm, D_MODEL), lambda i, t: (i, t, 0))
    col = pl.BlockSpec((1, D_MODEL, tm), lambda i, t: (i, 0, t))
    return pl.pallas_call(
        functools.partial(_qkv_prompt_kernel, tm=tm),
        grid=(b, s // tm),
        in_specs=[row, _resident(g.shape), _resident(w_q.shape), _resident(w_kt.shape), _resident(w_v.shape)],
        out_specs=[row, col, row, col, row],
        out_shape=[jax.ShapeDtypeStruct((b, s, D_MODEL), BF16),
                   jax.ShapeDtypeStruct((b, D_MODEL, s), F32),
                   jax.ShapeDtypeStruct((b, s, D_MODEL), F32),
                   jax.ShapeDtypeStruct((b, D_MODEL, s), BF16),
                   jax.ShapeDtypeStruct((b, s, D_MODEL), BF16)],
        scratch_shapes=[pltpu.VMEM((tm, V7X_LANES), F32), pltpu.VMEM((tm, V7X_LANES), F32),
                        pltpu.VMEM((HEAD_DIM // 2, tm), F32), pltpu.VMEM((HEAD_DIM // 2, tm), F32)],
        compiler_params=_params("arbitrary", "arbitrary"),
        name="qkv_rope_prompt",
    )(y, g, w_q, w_kt, w_v)


def _qkv_sample_kernel(y_ref, g_ref, wq_ref, wk_ref, wv_ref, q_ref, k_ref, v_ref, *, pos):
    f_l = _inv_freq(lax.broadcasted_iota(jnp.int32, (1, V7X_LANES), 1))
    cos, sin = jnp.cos(float(pos) * f_l), jnp.sin(float(pos) * f_l)
    h = _rms(y_ref[...], g_ref[...]).astype(BF16)
    q = _rope_lanes(_dot(h, wq_ref[...]), cos, sin)
    q_ref[...] = (q * (HEAD_DIM ** -0.5 * LOG2E)).astype(BF16)
    k_ref[...] = _rope_lanes(_dot(h, wk_ref[...]), cos, sin)
    v_ref[...] = _dot(h, wv_ref[...])


def _qkv_sample(y, g, w_q, w_k, w_v, pos):
    n = y.shape[0]
    args = (y, g, w_q, w_k, w_v)
    return pl.pallas_call(
        functools.partial(_qkv_sample_kernel, pos=pos),
        grid=(1,),
        in_specs=[_resident(a.shape) for a in args],
        out_specs=[_whole((n, D_MODEL))] * 3,
        out_shape=[jax.ShapeDtypeStruct((n, D_MODEL), BF16),
                   jax.ShapeDtypeStruct((n, D_MODEL), F32),
                   jax.ShapeDtypeStruct((n, D_MODEL), F32)],
        compiler_params=_params("arbitrary"),
        name="qkv_rope_sample",
    )(*args)


def _diff_lambda(lam_p, lam_init):
    a = jnp.sum(lam_p[0:1] * lam_p[1:2], axis=-1, keepdims=True)
    b = jnp.sum(lam_p[2:3] * lam_p[3:4], axis=-1, keepdims=True)
    return jnp.exp(a) - jnp.exp(b) + lam_init


def _prompt_attn_kernel(q_ref, kt_ref, v_ref, lam_ref, g_ref, o_ref, *, tq, tk, lam_init):
    qi = pl.program_id(2)
    q = q_ref[0]
    qc = (q[:, :HEAD_DIM], q[:, HEAD_DIM:])

    def kv_step(start, width, carry, masked):
        start = pl.multiple_of(start, tq)
        vblk = v_ref[0, pl.ds(start, width), :]
        new = []
        for c in range(2):
            m, l, acc = carry[c]
            s = _dot(qc[c], kt_ref[0, c * HEAD_DIM:(c + 1) * HEAD_DIM, pl.ds(start, width)])
            if masked:
                row = lax.broadcasted_iota(jnp.int32, (tq, width), 0)
                col = lax.broadcasted_iota(jnp.int32, (tq, width), 1)
                s = jnp.where(col <= row, s, NEG_INF)
            m_new = jnp.maximum(m, jnp.max(s, axis=-1, keepdims=True))
            alpha = jnp.exp2(m - m_new)
            p = jnp.exp2(s - m_new)
            l = alpha * l + jnp.sum(p, axis=-1, keepdims=True)
            acc = alpha * acc + _dot(p.astype(BF16), vblk)
            new.append((m_new, l, acc))
        return tuple(new)

    carry = tuple((jnp.full((tq, 1), NEG_INF, F32), jnp.zeros((tq, 1), F32), jnp.zeros((tq, V_DIM), F32))
                  for _ in range(2))
    n_wide = (qi * tq) // tk
    carry = lax.fori_loop(0, n_wide, lambda j, c: kv_step(j * tk, tk, c, False), carry)
    if tk > tq:
        carry = lax.fori_loop(n_wide * (tk // tq), qi, lambda j, c: kv_step(j * tq, tq, c, False), carry)
    (_, l1, a1), (_, l2, a2) = kv_step(qi * tq, tq, carry, True)
    lam = _diff_lambda(lam_ref[...], lam_init)
    o = a1 / l1 - lam * (a2 / l2)
    o_ref[0] = (_rms(o, g_ref[...]) * (1.0 - lam_init)).astype(BF16)


def _prompt_attention(q, ktb, vb, lam_p, subln_g, lam_init, tq=ROW_TILE, tk=KV_TILE):
    b, s, _ = q.shape
    tq = min(tq, s)
    tk = max(min(tk, s), tq)
    assert s % tk == 0 and tk % tq == 0
    return pl.pallas_call(
        functools.partial(_prompt_attn_kernel, tq=tq, tk=tk, lam_init=lam_init),
        grid=(b, N_HEADS, s // tq),
        in_specs=[pl.BlockSpec((1, tq, V_DIM), lambda i, h, t: (i, t, h)),
                  pl.BlockSpec((1, V_DIM, s), lambda i, h, t: (i, h, 0)),
                  pl.BlockSpec((1, s, V_DIM), lambda i, h, t: (i, 0, h)),
                  pl.BlockSpec(lam_p.shape, lambda i, h, t: (0, 0)),
                  pl.BlockSpec(subln_g.shape, lambda i, h, t: (0, 0))],
        out_specs=pl.BlockSpec((1, tq, V_DIM), lambda i, h, t: (i, t, h)),
        out_shape=jax.ShapeDtypeStruct((b, s, D_MODEL), BF16),
        compiler_params=_params("arbitrary", "arbitrary", "arbitrary"),
        name="prompt_diff_attention",
    )(q, ktb, vb, lam_p, subln_g)


def _decode_attn_kernel(pt_ref, q_ref, ks_ref, vs_ref, *rest, n_pg, lam_init):
    kt_refs, v_refs = rest[:n_pg], rest[n_pg:2 * n_pg]
    lam_ref, g_ref, o_ref, qblk, m_s, l_s, acc_s = rest[2 * n_pg:]
    p = pl.program_id(1)

    @pl.when(p == 0)
    def _():
        rowi = lax.broadcasted_iota(jnp.int32, (N_MAPS, D_MODEL), 0)
        coli = lax.broadcasted_iota(jnp.int32, (N_MAPS, D_MODEL), 1)
        own_k = (coli // HEAD_DIM) == ((rowi % N_HEADS) * 2 + rowi // N_HEADS)
        qb = jnp.where(own_k, jnp.broadcast_to(q_ref[0].astype(F32), (N_MAPS, D_MODEL)), 0.0)
        qblk[...] = qb.astype(BF16)
        m_s[...] = jnp.sum(qb * ks_ref[0], axis=-1, keepdims=True)
        l_s[...] = jnp.ones((N_MAPS, 1), F32)
        acc_s[...] = jnp.broadcast_to(vs_ref[0], (N_MAPS, D_MODEL))

    s = jnp.concatenate([_dot(qblk[...], kt[0].astype(BF16)) for kt in kt_refs], axis=-1)
    m_new = jnp.maximum(m_s[...], jnp.max(s, axis=-1, keepdims=True))
    alpha = jnp.exp2(m_s[...] - m_new)
    pr = jnp.exp2(s - m_new)
    l_s[...] = alpha * l_s[...] + jnp.sum(pr, axis=-1, keepdims=True)
    m_s[...] = m_new
    pr = pr.astype(BF16)
    cols = []
    for h in range(N_HEADS):
        acc_h = None
        for g, v_ref in enumerate(v_refs):
            vh = v_ref[0, pl.ds(h, PAGE_SIZE, stride=N_HEADS), :].astype(BF16)
            part = _dot(pr[:, g * PAGE_SIZE:(g + 1) * PAGE_SIZE], vh)
            acc_h = part if acc_h is None else acc_h + part
        cols.append(acc_h)
    acc_s[...] = alpha * acc_s[...] + jnp.concatenate(cols, axis=-1)

    @pl.when(p == pl.num_programs(1) - 1)
    def _():
        o_all = acc_s[...] / l_s[...]
        lam = _diff_lambda(lam_ref[...], lam_init)
        diff = o_all[:N_HEADS] - lam * o_all[N_HEADS:]
        own_v = (lax.broadcasted_iota(jnp.int32, (N_HEADS, D_MODEL), 1) // V_DIM
                 == lax.broadcasted_iota(jnp.int32, (N_HEADS, D_MODEL), 0))
        diff = jnp.where(own_v, diff, 0.0)
        ms = jnp.sum(diff * diff, axis=-1, keepdims=True) * (1.0 / V_DIM)
        normed = diff * lax.rsqrt(ms + NORM_EPS)
        o = jnp.sum(normed, axis=0, keepdims=True) * g_ref[...] * (1.0 - lam_init)
        o_ref[0] = o.astype(BF16)


def _decode_attention(q, k_self, v_self, cache_kt, cache_v2d, page_table, lam_p, subln_row, lam_init):
    n, n_pages = page_table.shape
    n_pg = math.gcd(PAGES_PER_STEP, n_pages)

    def page_spec(block, g):
        return pl.BlockSpec(block, lambda i, p, pt: (pt[i * n_pages + p * n_pg + g], 0, 0))

    one = pl.BlockSpec((1, 1, D_MODEL), lambda i, p, pt: (i, 0, 0))
    return pl.pallas_call(
        functools.partial(_decode_attn_kernel, n_pg=n_pg, lam_init=lam_init),
        grid_spec=pltpu.PrefetchScalarGridSpec(
            num_scalar_prefetch=1,
            grid=(n, n_pages // n_pg),
            in_specs=[one, one, one]
                     + [page_spec((1, D_MODEL, PAGE_SIZE), g) for g in range(n_pg)]
                     + [page_spec((1, PAGE_SIZE * N_HEADS, V_DIM), g) for g in range(n_pg)]
                     + [pl.BlockSpec(lam_p.shape, lambda i, p, pt: (0, 0)),
                        pl.BlockSpec(subln_row.shape, lambda i, p, pt: (0, 0))],
            out_specs=one,
            scratch_shapes=[pltpu.VMEM((N_MAPS, D_MODEL), BF16), pltpu.VMEM((N_MAPS, 1), F32),
                            pltpu.VMEM((N_MAPS, 1), F32), pltpu.VMEM((N_MAPS, D_MODEL), F32)]),
        out_shape=jax.ShapeDtypeStruct((n, 1, D_MODEL), BF16),
        compiler_params=_params("arbitrary", "arbitrary"),
        name="decode_diff_attention",
    )(page_table.reshape(-1), q, k_self, v_self, *([cache_kt] * n_pg), *([cache_v2d] * n_pg), lam_p, subln_row)


def _attn_out_kernel(y_ref, o_ref, w_o_ref, g_post_ref, gm_pre_ref, gm_post_ref, w_up_ref, w_down_ref, out_ref):
    y1 = y_ref[...] + _rms(_dot(o_ref[...], w_o_ref[...]), g_post_ref[...])
    out_ref[...] = _mlp_residual(y1, gm_pre_ref[...], gm_post_ref[...], w_up_ref, w_down_ref)


def _attn_out(y2d, o2d, w_o, g_post, gm_pre, gm_post, w_up, w_down):
    m = y2d.shape[0]
    tm = min(ROW_TILE, m)
    assert m % tm == 0
    row = pl.BlockSpec((tm, D_MODEL), lambda i: (i, 0))
    weights = (w_o, g_post, gm_pre, gm_post, w_up, w_down)
    return pl.pallas_call(
        _attn_out_kernel,
        grid=(m // tm,),
        in_specs=[row, row] + [_resident(a.shape) for a in weights],
        out_specs=row,
        out_shape=jax.ShapeDtypeStruct((m, D_MODEL), F32),
        compiler_params=_params("arbitrary"),
        name="attn_out_mlp",
    )(y2d, o2d, *weights)


def kernel(x_prompt, x_sample, state_conv, state_pool, cache_k, cache_v, page_table, norm_mix_pre, norm_mix_post, norm_mlp_pre, norm_mlp_post, mlp_up, mlp_down, sc_w_in, sc_conv_w, pool_w, pool_scale, sc_w_out, attn_w_qkv, attn_lambda, attn_subln, attn_w_o):
    b, s, _ = x_prompt.shape
    n, t_dec, _ = x_sample.shape
    assert t_dec == 1 and norm_mix_pre.shape[0] == 2
    past_len = page_table.shape[1] * PAGE_SIZE
    lam_init = 0.8 - 0.6 * math.exp(-0.3 * 1)

    def gain(a, layer):
        return a[layer].reshape(1, D_MODEL)

    bf = lambda a: a.astype(BF16)
    l0 = (gain(norm_mix_pre, 0), gain(norm_mix_post, 0), bf(sc_w_in[0]), sc_conv_w[0], bf(pool_w[0]),
          pool_scale[0].reshape(1, D_POOL), bf(sc_w_out[0]), gain(norm_mlp_pre, 0), gain(norm_mlp_post, 0),
          bf(mlp_up[0]), bf(mlp_down[0]))
    l1_out = (bf(attn_w_o[0]), gain(norm_mix_post, 1), gain(norm_mlp_pre, 1), gain(norm_mlp_post, 1),
              bf(mlp_up[1]), bf(mlp_down[1]))
    w_qkv = bf(attn_w_qkv[0])
    w_q, w_k, w_v = w_qkv[:, :D_MODEL], w_qkv[:, D_MODEL:2 * D_MODEL], w_qkv[:, 2 * D_MODEL:]
    g_qkv = gain(norm_mix_pre, 1)
    lam_p = attn_lambda[0]
    subln = attn_subln[0].reshape(1, V_DIM)

    yp, conv_p, pool_p = _mixer0_prompt(x_prompt, *l0)
    ys, conv_s, pool_s = _mixer0_sample(x_sample.reshape(n, D_MODEL),
                                        jnp.swapaxes(state_conv[0], 0, 1), jnp.swapaxes(state_pool[0], 0, 1),
                                        past_len, *l0)

    qp, ktp, vp, ktbp, vbp = _qkv_prompt(yp, g_qkv, w_q, w_k.T, w_v)
    qs, ks, vs = _qkv_sample(ys, g_qkv, w_q, w_k, w_v, past_len)
    op = _prompt_attention(qp, ktbp, vbp, lam_p, subln, lam_init)
    n_pool = cache_k.shape[1]
    cache_kt = jnp.transpose(cache_k[0], (0, 2, 3, 4, 1)).reshape(n_pool, D_MODEL, PAGE_SIZE)
    cache_v2d = cache_v[0].reshape(n_pool, PAGE_SIZE * N_HEADS, V_DIM)
    os_ = _decode_attention(qs.reshape(n, 1, D_MODEL), ks.reshape(n, 1, D_MODEL), vs.reshape(n, 1, D_MODEL),
                            cache_kt, cache_v2d, page_table, lam_p, jnp.tile(subln, (1, N_HEADS)), lam_init)
    yp = _attn_out(yp.reshape(b * s, D_MODEL), op.reshape(b * s, D_MODEL), *l1_out)
    ys = _attn_out(ys, os_.reshape(n, D_MODEL), *l1_out)

    k_rows_p = jnp.transpose(ktp.reshape(b, N_HEADS, 2, HEAD_DIM, s), (0, 4, 1, 2, 3))
    return (yp.reshape(b, s, D_MODEL), ys.reshape(n, 1, D_MODEL),
            conv_p[None], jnp.swapaxes(conv_s, 0, 1)[None],
            pool_p[None], jnp.swapaxes(pool_s, 0, 1)[None],
            k_rows_p[None], vp.reshape(1, b, s, N_HEADS, V_DIM),
            ks.reshape(1, n, 1, N_HEADS, 2, HEAD_DIM), vs.reshape(1, n, 1, N_HEADS, V_DIM))
```

```python
import functools
import math

import jax
import jax.numpy as jnp
from jax import lax
from jax.experimental import pallas as pl
from jax.experimental.pallas import tpu as pltpu

D_MODEL = 1024
D_CONV = D_MODEL // 2
D_POOL = D_MODEL // 2
CONV_WIDTH = 3
POOL_WINDOWS = (2, 4, 8, 16)
POOL_GC = D_POOL // len(POOL_WINDOWS)
POOL_BUF = max(POOL_WINDOWS) - 1
HEAD_DIM = 64
N_HEADS = D_MODEL // (2 * HEAD_DIM)
N_MAPS = 2 * N_HEADS
V_DIM = 2 * HEAD_DIM
D_FF = 4 * D_MODEL
ROPE_THETA = 10000.0
NORM_EPS = 1e-6
PAGE_SIZE = 128
NEG_INF = -1e30
LOG2E = math.log2(math.e)

V7X_LANES = 128
V7X_SUBLANES = 8
V7X_VMEM_BYTES = 64 * 1024 * 1024
VMEM_LIMIT_BYTES = V7X_VMEM_BYTES - 8 * 1024 * 1024

CONV_HIST = V7X_SUBLANES
POOL_HIST = 2 * V7X_SUBLANES
ROW_TILE = 512
Q_TILE = 1024
KV_TILE = 2048
SUB_TILE_WIDE = 512
SUB_TILE_NARROW = 256
PAGES_PER_STEP = 8

F32 = jnp.float32
BF16 = jnp.bfloat16


def _resident(shape):
    nd = len(shape)
    return pl.BlockSpec(shape, lambda *_: (0,) * nd, pipeline_mode=pl.Buffered(1))


def _whole(shape):
    nd = len(shape)
    return pl.BlockSpec(shape, lambda *_: (0,) * nd)


def _params(*sem):
    return pltpu.CompilerParams(dimension_semantics=sem, vmem_limit_bytes=VMEM_LIMIT_BYTES)


def _rms(x, g):
    ms = jnp.mean(x * x, axis=-1, keepdims=True)
    return x * lax.rsqrt(ms + NORM_EPS) * g


def _dot(a, b):
    return jnp.dot(a, b, preferred_element_type=F32)


def _dot_nt(a, b):
    return lax.dot_general(a, b, (((1,), (1,)), ((), ())), preferred_element_type=F32)


def _mlp_residual(y, g_pre, g_post, w_up_ref, w_down_ref):
    h = _rms(y, g_pre).astype(BF16)
    r = jnp.maximum(_dot(h, w_up_ref[...]), 0.0)
    out = _dot((r * r).astype(BF16), w_down_ref[...])
    return y + _rms(out, g_post)


def _conv_pool_outputs(proj, conv_prev, pool_prev, pos, conv_w, pool_w_ref, pool_scale, w_out_ref):
    b_gate = proj[:, :D_CONV]
    conv_in = proj[:, D_CONV:2 * D_CONV] * proj[:, 2 * D_CONV:3 * D_CONV]
    u = proj[:, 3 * D_CONV:]
    conv_out = conv_w[CONV_WIDTH - 1:CONV_WIDTH] * conv_in
    for k in range(1, CONV_WIDTH):
        conv_out = conv_out + conv_w[CONV_WIDTH - 1 - k:CONV_WIDTH - k] * conv_prev(k)
    y_a = b_gate * conv_out
    ybs = []
    for g, w in enumerate(POOL_WINDOWS):
        sl = slice(g * POOL_GC, (g + 1) * POOL_GC)
        ug = u[:, sl]
        s = ug
        for j in range(1, w):
            s = s + pool_prev(j, sl)
        if pos is None:
            d = s / float(w) - ug
        else:
            d = s / jnp.minimum(pos + 1, w).astype(F32) - ug
        ybs.append(_dot(d.astype(BF16), pool_w_ref[g]))
    y_b = jnp.concatenate(ybs, axis=-1) * pool_scale
    mix = jnp.concatenate([y_a, y_b], axis=-1).astype(BF16)
    return _dot(mix, w_out_ref[...]), conv_in, u


def _mixer0_prompt_kernel(x_ref, g_pre_ref, g_post_ref, w_in_ref, conv_w_ref, pool_w_ref, pool_scale_ref,
                          w_out_ref, gm_pre_ref, gm_post_ref, w_up_ref, w_down_ref,
                          y_ref, conv_state_ref, pool_state_ref, conv_ext, pool_ext, *, tm):
    t = pl.program_id(1)

    @pl.when(t == 0)
    def _():
        conv_ext[0:CONV_HIST, :] = jnp.zeros((CONV_HIST, D_CONV), F32)
        pool_ext[0:POOL_HIST, :] = jnp.zeros((POOL_HIST, D_POOL), F32)

    x = x_ref[0]
    h = _rms(x, g_pre_ref[...]).astype(BF16)
    proj = _dot(h, w_in_ref[...])
    conv_ext[CONV_HIST:CONV_HIST + tm, :] = proj[:, D_CONV:2 * D_CONV] * proj[:, 2 * D_CONV:3 * D_CONV]
    pool_ext[POOL_HIST:POOL_HIST + tm, :] = proj[:, 3 * D_CONV:]
    pos = t * tm + lax.broadcasted_iota(jnp.int32, (tm, 1), 0)
    out, _, _ = _conv_pool_outputs(
        proj,
        lambda k: conv_ext[CONV_HIST - k:CONV_HIST - k + tm, :],
        lambda j, sl: pool_ext[POOL_HIST - j:POOL_HIST - j + tm, sl],
        pos, conv_w_ref[...], pool_w_ref, pool_scale_ref[...], w_out_ref)
    y1 = x + _rms(out, g_post_ref[...])
    y_ref[0] = _mlp_residual(y1, gm_pre_ref[...], gm_post_ref[...], w_up_ref, w_down_ref)

    @pl.when(t == pl.num_programs(1) - 1)
    def _():
        conv_state_ref[0] = conv_ext[CONV_HIST + tm - (CONV_WIDTH - 1):CONV_HIST + tm, :]
        pool_state_ref[0] = pool_ext[POOL_HIST + tm - POOL_BUF:POOL_HIST + tm, :]

    conv_ext[0:CONV_HIST, :] = conv_ext[tm:tm + CONV_HIST, :]
    pool_ext[0:POOL_HIST, :] = pool_ext[tm:tm + POOL_HIST, :]


def _mixer0_prompt(x, g_pre, g_post, w_in, conv_w, pool_w, pool_scale, w_out, gm_pre, gm_post, w_up, w_down):
    b, s, _ = x.shape
    tm = min(ROW_TILE, s)
    assert s % tm == 0 and tm >= POOL_HIST
    row = pl.BlockSpec((1, tm, D_MODEL), lambda i, t: (i, t, 0))
    return pl.pallas_call(
        functools.partial(_mixer0_prompt_kernel, tm=tm),
        grid=(b, s // tm),
        in_specs=[row, _resident(g_pre.shape), _resident(g_post.shape), _resident(w_in.shape),
                  _resident(conv_w.shape), _resident(pool_w.shape), _resident(pool_scale.shape),
                  _resident(w_out.shape), _resident(gm_pre.shape), _resident(gm_post.shape),
                  _resident(w_up.shape), _resident(w_down.shape)],
        out_specs=[row,
                   pl.BlockSpec((1, CONV_WIDTH - 1, D_CONV), lambda i, t: (i, 0, 0)),
                   pl.BlockSpec((1, POOL_BUF, D_POOL), lambda i, t: (i, 0, 0))],
        out_shape=[jax.ShapeDtypeStruct((b, s, D_MODEL), F32),
                   jax.ShapeDtypeStruct((b, CONV_WIDTH - 1, D_CONV), F32),
                   jax.ShapeDtypeStruct((b, POOL_BUF, D_POOL), F32)],
        scratch_shapes=[pltpu.VMEM((CONV_HIST + tm, D_CONV), F32), pltpu.VMEM((POOL_HIST + tm, D_POOL), F32)],
        compiler_params=_params("arbitrary", "arbitrary"),
        name="mixer0_prompt",
    )(x, g_pre, g_post, w_in, conv_w, pool_w, pool_scale, w_out, gm_pre, gm_post, w_up, w_down)


def _mixer0_sample_kernel(x_ref, conv_hist_ref, pool_hist_ref, g_pre_ref, g_post_ref, w_in_ref, conv_w_ref,
                          pool_w_ref, pool_scale_ref, w_out_ref, gm_pre_ref, gm_post_ref, w_up_ref, w_down_ref,
                          y_ref, conv_state_ref, pool_state_ref, *, full_windows):
    x = x_ref[...]
    h = _rms(x, g_pre_ref[...]).astype(BF16)
    proj = _dot(h, w_in_ref[...])
    assert full_windows
    out, conv_in, u = _conv_pool_outputs(
        proj,
        lambda k: conv_hist_ref[CONV_WIDTH - 1 - k],
        lambda j, sl: pool_hist_ref[POOL_BUF - j][:, sl],
        None, conv_w_ref[...], pool_w_ref, pool_scale_ref[...], w_out_ref)
    y1 = x + _rms(out, g_post_ref[...])
    y_ref[...] = _mlp_residual(y1, gm_pre_ref[...], gm_post_ref[...], w_up_ref, w_down_ref)
    for k in range(CONV_WIDTH - 2):
        conv_state_ref[k] = conv_hist_ref[k + 1]
    conv_state_ref[CONV_WIDTH - 2] = conv_in
    for j in range(POOL_BUF - 1):
        pool_state_ref[j] = pool_hist_ref[j + 1]
    pool_state_ref[POOL_BUF - 1] = u


def _mixer0_sample(x, conv_hist, pool_hist, past_len, *weights):
    n = x.shape[0]
    args = (x, conv_hist, pool_hist) + weights
    return pl.pallas_call(
        functools.partial(_mixer0_sample_kernel, full_windows=past_len + 1 >= max(POOL_WINDOWS)),
        grid=(1,),
        in_specs=[_resident(a.shape) for a in args],
        out_specs=[_whole((n, D_MODEL)), _whole(conv_hist.shape), _whole(pool_hist.shape)],
        out_shape=[jax.ShapeDtypeStruct((n, D_MODEL), F32),
                   jax.ShapeDtypeStruct(conv_hist.shape, F32),
                   jax.ShapeDtypeStruct(pool_hist.shape, F32)],
        compiler_params=_params("arbitrary"),
        name="mixer0_sample",
    )(*args)


def _inv_freq(idx):
    half = HEAD_DIM // 2
    return jnp.exp((idx % half).astype(F32) * (-math.log(ROPE_THETA) / half))


def _rope_lanes(x, cos, sin):
    lane = lax.broadcasted_iota(jnp.int32, (1, V7X_LANES), 1)
    first_half = (lane % HEAD_DIM) < HEAD_DIM // 2
    sin_signed = jnp.where(first_half, -sin, sin)
    outs = []
    for j in range(D_MODEL // V7X_LANES):
        xj = x[:, j * V7X_LANES:(j + 1) * V7X_LANES]
        partner = jnp.where(first_half,
                            pltpu.roll(xj, V7X_LANES - HEAD_DIM // 2, 1),
                            pltpu.roll(xj, HEAD_DIM // 2, 1))
        outs.append(xj * cos + partner * sin_signed)
    return jnp.concatenate(outs, axis=-1)


def _rope_rows(xt, cos, sin):
    half = HEAD_DIM // 2
    outs = []
    for g in range(D_MODEL // HEAD_DIM):
        x1 = xt[g * HEAD_DIM:g * HEAD_DIM + half]
        x2 = xt[g * HEAD_DIM + half:(g + 1) * HEAD_DIM]
        outs += [x1 * cos - x2 * sin, x2 * cos + x1 * sin]
    return jnp.concatenate(outs, axis=0)


def _qkv_prompt_kernel(y_ref, g_ref, wqt_ref, wkt_ref, wv_ref, qt_ref, kt_ref, kb_ref, v_ref, vtb_ref,
                       cos_t, sin_t, *, tm):
    t = pl.program_id(1)
    half = HEAD_DIM // 2
    freq = _inv_freq(lax.broadcasted_iota(jnp.int32, (half, 1), 0))

    @pl.when((pl.program_id(0) == 0) & (t == 0))
    def _():
        r = lax.broadcasted_iota(jnp.int32, (half, tm), 1).astype(F32)
        cos_t[...] = jnp.cos(r * freq)
        sin_t[...] = jnp.sin(r * freq)

    t0 = (t * tm).astype(F32)
    cb, sb = jnp.cos(t0 * freq), jnp.sin(t0 * freq)
    cos = cb * cos_t[...] - sb * sin_t[...]
    sin = sb * cos_t[...] + cb * sin_t[...]

    h = _rms(y_ref[0], g_ref[...]).astype(BF16)
    qt = _rope_rows(_dot_nt(wqt_ref[...], h), cos, sin)
    qt_ref[0] = (qt * (HEAD_DIM ** -0.5 * LOG2E)).astype(BF16)
    kt = _rope_rows(_dot_nt(wkt_ref[...], h), cos, sin)
    kt_ref[0] = kt
    kb_ref[0] = kt.T.astype(BF16)
    v = _dot(h, wv_ref[...])
    v_ref[0] = v
    vtb_ref[0] = v.T.astype(BF16)


def _qkv_prompt(y, g, w_q, w_k, w_v):
    b, s, _ = y.shape
    tm = min(ROW_TILE, s)
    assert s % tm == 0
    row = pl.BlockSpec((1, tm, D_MODEL), lambda i, t: (i, t, 0))
    col = pl.BlockSpec((1, D_MODEL, tm), lambda i, t: (i, 0, t))
    weights = (w_q.T, w_k.T, w_v)
    return pl.pallas_call(
        functools.partial(_qkv_prompt_kernel, tm=tm),
        grid=(b, s // tm),
        in_specs=[row, _resident(g.shape)] + [_resident(w.shape) for w in weights],
        out_specs=[col, col, row, row, col],
        out_shape=[jax.ShapeDtypeStruct((b, D_MODEL, s), BF16),
                   jax.ShapeDtypeStruct((b, D_MODEL, s), F32),
                   jax.ShapeDtypeStruct((b, s, D_MODEL), BF16),
                   jax.ShapeDtypeStruct((b, s, D_MODEL), F32),
                   jax.ShapeDtypeStruct((b, D_MODEL, s), BF16)],
        scratch_shapes=[pltpu.VMEM((HEAD_DIM // 2, tm), F32), pltpu.VMEM((HEAD_DIM // 2, tm), F32)],
        compiler_params=_params("arbitrary", "arbitrary"),
        name="qkv_rope_prompt",
    )(y, g, *weights)


def _qkv_sample_kernel(y_ref, g_ref, wq_ref, wk_ref, wv_ref, q_ref, k_ref, v_ref, *, pos):
    f_l = _inv_freq(lax.broadcasted_iota(jnp.int32, (1, V7X_LANES), 1))
    cos, sin = jnp.cos(float(pos) * f_l), jnp.sin(float(pos) * f_l)
    h = _rms(y_ref[...], g_ref[...]).astype(BF16)
    q = _rope_lanes(_dot(h, wq_ref[...]), cos, sin)
    q_ref[...] = (q * (HEAD_DIM ** -0.5 * LOG2E)).astype(BF16)
    k_ref[...] = _rope_lanes(_dot(h, wk_ref[...]), cos, sin)
    v_ref[...] = _dot(h, wv_ref[...])


def _qkv_sample(y, g, w_q, w_k, w_v, pos):
    n = y.shape[0]
    args = (y, g, w_q, w_k, w_v)
    return pl.pallas_call(
        functools.partial(_qkv_sample_kernel, pos=pos),
        grid=(1,),
        in_specs=[_resident(a.shape) for a in args],
        out_specs=[_whole((n, D_MODEL))] * 3,
        out_shape=[jax.ShapeDtypeStruct((n, D_MODEL), BF16),
                   jax.ShapeDtypeStruct((n, D_MODEL), F32),
                   jax.ShapeDtypeStruct((n, D_MODEL), F32)],
        compiler_params=_params("arbitrary"),
        name="qkv_rope_sample",
    )(*args)


def _diff_lambda(lam_p, lam_init):
    a = jnp.sum(lam_p[0:1] * lam_p[1:2], axis=-1, keepdims=True)
    b = jnp.sum(lam_p[2:3] * lam_p[3:4], axis=-1, keepdims=True)
    return jnp.exp(a) - jnp.exp(b) + lam_init


def _prompt_attn_kernel(qt_ref, k_ref, vt_ref, lam_ref, g_ref, o_ref, *, tq, tk, sub_wide, sub_narrow, lam_init):
    qi = pl.program_id(2)
    qt = qt_ref[0]
    row_map = lax.broadcasted_iota(jnp.int32, (V_DIM, tq), 0) // HEAD_DIM
    qz = tuple(jnp.where(row_map == c, qt, jnp.zeros_like(qt)) for c in range(2))

    def kv_step(start, width, sub, carry, masked):
        start = pl.multiple_of(start, tq)
        items = [(u, c) for u in range(width // sub) for c in range(2)]
        state = [list(carry[0]), list(carry[1])]
        scores, probs = {}, {}

        def first_lane(u):
            return u * sub if masked else 0

        def update_lanes(old, off, new):
            return new if off == 0 else jnp.concatenate([old[:, :off], new], axis=1)

        def stage_scores(u, c):
            off = first_lane(u)
            kblk = k_ref[0, pl.ds(start + u * sub, sub), :]
            s = _dot(kblk, qz[c][:, off:])
            if masked:
                key = lax.broadcasted_iota(jnp.int32, s.shape, 0)
                qry = lax.broadcasted_iota(jnp.int32, s.shape, 1)
                s = jnp.where(key <= qry, s, NEG_INF)
            scores[(u, c)] = s

        def stage_softmax(u, c):
            off = first_lane(u)
            s = scores.pop((u, c))
            m, l, _ = state[c]
            m_new = jnp.maximum(m[:, off:], jnp.max(s, axis=0, keepdims=True))
            alpha = jnp.exp2(m[:, off:] - m_new)
            p = jnp.exp2(s - m_new)
            state[c][0] = update_lanes(m, off, m_new)
            state[c][1] = update_lanes(l, off, alpha * l[:, off:] + jnp.sum(p, axis=0, keepdims=True))
            probs[(u, c)] = (alpha, p.astype(BF16))

        def stage_values(u, c):
            off = first_lane(u)
            alpha, p = probs.pop((u, c))
            vtblk = vt_ref[0, :, pl.ds(start + u * sub, sub)]
            acc = state[c][2]
            state[c][2] = update_lanes(acc, off, alpha * acc[:, off:] + _dot(vtblk, p))

        for t in range(len(items) + 2):
            if t < len(items):
                stage_scores(*items[t])
            if 0 <= t - 1 < len(items):
                stage_softmax(*items[t - 1])
            if 0 <= t - 2 < len(items):
                stage_values(*items[t - 2])
        return tuple(tuple(st) for st in state)

    carry = tuple((jnp.full((1, tq), NEG_INF, F32), jnp.zeros((1, tq), F32), jnp.zeros((V_DIM, tq), F32))
                  for _ in range(2))
    n_wide = (qi * tq) // tk

    carry = lax.fori_loop(0, n_wide, lambda j, c: kv_step(j * tk, tk, sub_wide, c, False), carry)
    if tk > tq:
        carry = lax.fori_loop(n_wide * (tk // tq), qi,
                              lambda j, c: kv_step(j * tq, tq, sub_narrow, c, False), carry)
    (_, l1, a1), (_, l2, a2) = kv_step(qi * tq, tq, sub_narrow, carry, True)
    lam = _diff_lambda(lam_ref[...], lam_init)
    ot = a1 / l1 - lam * (a2 / l2)
    ms = jnp.mean(ot * ot, axis=0, keepdims=True)
    ot = ot * lax.rsqrt(ms + NORM_EPS) * (g_ref[...] * (1.0 - lam_init))
    o_ref[0] = ot.T.astype(BF16)


def _prompt_attention(qt, kb, vt, lam_p, subln_col, lam_init, tq=Q_TILE, tk=KV_TILE,
                      sub_wide=SUB_TILE_WIDE, sub_narrow=SUB_TILE_NARROW):
    b, _, s = qt.shape
    tq = min(tq, s)
    tk = max(min(tk, s), tq)
    assert s % tk == 0 and tk % tq == 0
    return pl.pallas_call(
        functools.partial(_prompt_attn_kernel, tq=tq, tk=tk, sub_wide=min(sub_wide, tk),
                          sub_narrow=min(sub_narrow, tq), lam_init=lam_init),
        grid=(b, N_HEADS, s // tq),
        in_specs=[pl.BlockSpec((1, V_DIM, tq), lambda i, h, t: (i, h, t)),
                  pl.BlockSpec((1, s, V_DIM), lambda i, h, t: (i, 0, h)),
                  pl.BlockSpec((1, V_DIM, s), lambda i, h, t: (i, h, 0)),
                  pl.BlockSpec(lam_p.shape, lambda i, h, t: (0, 0)),
                  pl.BlockSpec(subln_col.shape, lambda i, h, t: (0, 0))],
        out_specs=pl.BlockSpec((1, tq, V_DIM), lambda i, h, t: (i, t, h)),
        out_shape=jax.ShapeDtypeStruct((b, s, D_MODEL), BF16),
        compiler_params=_params("arbitrary", "arbitrary", "arbitrary"),
        name="prompt_diff_attention",
    )(qt, kb, vt, lam_p, subln_col)


def _decode_attn_kernel(pt_ref, q_ref, ks_ref, vs_ref, *rest, n_pg, lam_init):
    kt_refs, v_refs = rest[:n_pg], rest[n_pg:2 * n_pg]
    lam_ref, g_ref, o_ref, qblk, m_s, l_s, acc_s = rest[2 * n_pg:]
    p = pl.program_id(1)

    @pl.when(p == 0)
    def _():
        rowi = lax.broadcasted_iota(jnp.int32, (N_MAPS, D_MODEL), 0)
        coli = lax.broadcasted_iota(jnp.int32, (N_MAPS, D_MODEL), 1)
        own_k = (coli // HEAD_DIM) == ((rowi % N_HEADS) * 2 + rowi // N_HEADS)
        qb = jnp.where(own_k, jnp.broadcast_to(q_ref[0].astype(F32), (N_MAPS, D_MODEL)), 0.0)
        qblk[...] = qb.astype(BF16)
        m_s[...] = jnp.sum(qb * ks_ref[0], axis=-1, keepdims=True)
        l_s[...] = jnp.ones((N_MAPS, 1), F32)
        acc_s[...] = jnp.broadcast_to(vs_ref[0], (N_MAPS, D_MODEL))

    s = jnp.concatenate([_dot(qblk[...], kt[0].astype(BF16)) for kt in kt_refs], axis=-1)
    m_new = jnp.maximum(m_s[...], jnp.max(s, axis=-1, keepdims=True))
    alpha = jnp.exp2(m_s[...] - m_new)
    pr = jnp.exp2(s - m_new)
    l_s[...] = alpha * l_s[...] + jnp.sum(pr, axis=-1, keepdims=True)
    m_s[...] = m_new
    pr = pr.astype(BF16)
    cols = []
    for h in range(N_HEADS):
        acc_h = None
        for g, v_ref in enumerate(v_refs):
            vh = v_ref[0, pl.ds(h, PAGE_SIZE, stride=N_HEADS), :].astype(BF16)
            part = _dot(pr[:, g * PAGE_SIZE:(g + 1) * PAGE_SIZE], vh)
            acc_h = part if acc_h is None else acc_h + part
        cols.append(acc_h)
    acc_s[...] = alpha * acc_s[...] + jnp.concatenate(cols, axis=-1)

    @pl.when(p == pl.num_programs(1) - 1)
    def _():
        o_all = acc_s[...] / l_s[...]
        lam = _diff_lambda(lam_ref[...], lam_init)
        diff = o_all[:N_HEADS] - lam * o_all[N_HEADS:]
        own_v = (lax.broadcasted_iota(jnp.int32, (N_HEADS, D_MODEL), 1) // V_DIM
                 == lax.broadcasted_iota(jnp.int32, (N_HEADS, D_MODEL), 0))
        diff = jnp.where(own_v, diff, 0.0)
        ms = jnp.sum(diff * diff, axis=-1, keepdims=True) * (1.0 / V_DIM)
        normed = diff * lax.rsqrt(ms + NORM_EPS)
        o = jnp.sum(normed, axis=0, keepdims=True) * g_ref[...] * (1.0 - lam_init)
        o_ref[0] = o.astype(BF16)


def _decode_attention(q, k_self, v_self, cache_kt, cache_v2d, page_table, lam_p, subln_row, lam_init):
    n, n_pages = page_table.shape
    n_pg = math.gcd(PAGES_PER_STEP, n_pages)

    def page_spec(block, g):
        return pl.BlockSpec(block, lambda i, p, pt: (pt[i * n_pages + p * n_pg + g], 0, 0))

    one = pl.BlockSpec((1, 1, D_MODEL), lambda i, p, pt: (i, 0, 0))
    return pl.pallas_call(
        functools.partial(_decode_attn_kernel, n_pg=n_pg, lam_init=lam_init),
        grid_spec=pltpu.PrefetchScalarGridSpec(
            num_scalar_prefetch=1,
            grid=(n, n_pages // n_pg),
            in_specs=[one, one, one]
                     + [page_spec((1, D_MODEL, PAGE_SIZE), g) for g in range(n_pg)]
                     + [page_spec((1, PAGE_SIZE * N_HEADS, V_DIM), g) for g in range(n_pg)]
                     + [pl.BlockSpec(lam_p.shape, lambda i, p, pt: (0, 0)),
                        pl.BlockSpec(subln_row.shape, lambda i, p, pt: (0, 0))],
            out_specs=one,
            scratch_shapes=[pltpu.VMEM((N_MAPS, D_MODEL), BF16), pltpu.VMEM((N_MAPS, 1), F32),
                            pltpu.VMEM((N_MAPS, 1), F32), pltpu.VMEM((N_MAPS, D_MODEL), F32)]),
        out_shape=jax.ShapeDtypeStruct((n, 1, D_MODEL), BF16),
        compiler_params=_params("arbitrary", "arbitrary"),
        name="decode_diff_attention",
    )(page_table.reshape(-1), q, k_self, v_self, *([cache_kt] * n_pg), *([cache_v2d] * n_pg), lam_p, subln_row)


def _attn_out_kernel(y_ref, o_ref, w_o_ref, g_post_ref, gm_pre_ref, gm_post_ref, w_up_ref, w_down_ref, out_ref):
    y1 = y_ref[...] + _rms(_dot(o_ref[...], w_o_ref[...]), g_post_ref[...])
    out_ref[...] = _mlp_residual(y1, gm_pre_ref[...], gm_post_ref[...], w_up_ref, w_down_ref)


def _attn_out(y2d, o2d, w_o, g_post, gm_pre, gm_post, w_up, w_down):
    m = y2d.shape[0]
    tm = min(ROW_TILE, m)
    assert m % tm == 0
    row = pl.BlockSpec((tm, D_MODEL), lambda i: (i, 0))
    weights = (w_o, g_post, gm_pre, gm_post, w_up, w_down)
    return pl.pallas_call(
        _attn_out_kernel,
        grid=(m // tm,),
        in_specs=[row, row] + [_resident(a.shape) for a in weights],
        out_specs=row,
        out_shape=jax.ShapeDtypeStruct((m, D_MODEL), F32),
        compiler_params=_params("arbitrary"),
        name="attn_out_mlp",
    )(y2d, o2d, *weights)


def kernel(x_prompt, x_sample, state_conv, state_pool, cache_k, cache_v, page_table, norm_mix_pre, norm_mix_post, norm_mlp_pre, norm_mlp_post, mlp_up, mlp_down, sc_w_in, sc_conv_w, pool_w, pool_scale, sc_w_out, attn_w_qkv, attn_lambda, attn_subln, attn_w_o):
    b, s, _ = x_prompt.shape
    n, t_dec, _ = x_sample.shape
    assert t_dec == 1 and norm_mix_pre.shape[0] == 2
    past_len = page_table.shape[1] * PAGE_SIZE
    lam_init = 0.8 - 0.6 * math.exp(-0.3 * 1)

    def gain(a, layer):
        return a[layer].reshape(1, D_MODEL)

    bf = lambda a: a.astype(BF16)
    l0 = (gain(norm_mix_pre, 0), gain(norm_mix_post, 0), bf(sc_w_in[0]), sc_conv_w[0], bf(pool_w[0]),
          pool_scale[0].reshape(1, D_POOL), bf(sc_w_out[0]), gain(norm_mlp_pre, 0), gain(norm_mlp_post, 0),
          bf(mlp_up[0]), bf(mlp_down[0]))
    l1_out = (bf(attn_w_o[0]), gain(norm_mix_post, 1), gain(norm_mlp_pre, 1), gain(norm_mlp_post, 1),
              bf(mlp_up[1]), bf(mlp_down[1]))
    w_qkv = bf(attn_w_qkv[0])
    w_q, w_k, w_v = w_qkv[:, :D_MODEL], w_qkv[:, D_MODEL:2 * D_MODEL], w_qkv[:, 2 * D_MODEL:]
    g_qkv = gain(norm_mix_pre, 1)
    lam_p = attn_lambda[0]
    subln = attn_subln[0].reshape(1, V_DIM)

    yp, conv_p, pool_p = _mixer0_prompt(x_prompt, *l0)
    ys, conv_s, pool_s = _mixer0_sample(x_sample.reshape(n, D_MODEL),
                                        jnp.swapaxes(state_conv[0], 0, 1), jnp.swapaxes(state_pool[0], 0, 1),
                                        past_len, *l0)

    qtp, ktp, kbp, vp, vtbp = _qkv_prompt(yp, g_qkv, w_q, w_k, w_v)
    qs, ks, vs = _qkv_sample(ys, g_qkv, w_q, w_k, w_v, past_len)
    op = _prompt_attention(qtp, kbp, vtbp, lam_p, subln.reshape(V_DIM, 1), lam_init)
    n_pool = cache_k.shape[1]
    cache_kt = jnp.transpose(cache_k[0], (0, 2, 3, 4, 1)).reshape(n_pool, D_MODEL, PAGE_SIZE)
    cache_v2d = cache_v[0].reshape(n_pool, PAGE_SIZE * N_HEADS, V_DIM)
    os_ = _decode_attention(qs.reshape(n, 1, D_MODEL), ks.reshape(n, 1, D_MODEL), vs.reshape(n, 1, D_MODEL),
                            cache_kt, cache_v2d, page_table, lam_p, jnp.tile(subln, (1, N_HEADS)), lam_init)
    yp = _attn_out(yp.reshape(b * s, D_MODEL), op.reshape(b * s, D_MODEL), *l1_out)
    ys = _attn_out(ys, os_.reshape(n, D_MODEL), *l1_out)

    k_rows_p = jnp.transpose(ktp.reshape(b, N_HEADS, 2, HEAD_DIM, s), (0, 4, 1, 2, 3))
    return (yp.reshape(b, s, D_MODEL), ys.reshape(n, 1, D_MODEL),
            conv_p[None], jnp.swapaxes(conv_s, 0, 1)[None],
            pool_p[None], jnp.swapaxes(pool_s, 0, 1)[None],
            k_rows_p[None], vp.reshape(1, b, s, N_HEADS, V_DIM),
            ks.reshape(1, n, 1, N_HEADS, 2, HEAD_DIM), vs.reshape(1, n, 1, N_HEADS, V_DIM))
```

```python
import functools
import math

import jax
import jax.numpy as jnp
from jax import lax
from jax.experimental import pallas as pl
from jax.experimental.pallas import tpu as pltpu

D_MODEL = 1024
D_CONV = D_MODEL // 2
D_POOL = D_MODEL // 2
CONV_WIDTH = 3
POOL_WINDOWS = (2, 4, 8, 16)
POOL_GC = D_POOL // len(POOL_WINDOWS)
POOL_BUF = max(POOL_WINDOWS) - 1
HEAD_DIM = 64
N_HEADS = D_MODEL // (2 * HEAD_DIM)
N_MAPS = 2 * N_HEADS
V_DIM = 2 * HEAD_DIM
D_FF = 4 * D_MODEL
ROPE_THETA = 10000.0
NORM_EPS = 1e-6
PAGE_SIZE = 128
NEG_INF = -1e30
LOG2E = math.log2(math.e)

V7X_LANES = 128
V7X_SUBLANES = 8
V7X_VMEM_BYTES = 64 * 1024 * 1024
VMEM_LIMIT_BYTES = V7X_VMEM_BYTES - 8 * 1024 * 1024

CONV_HIST = V7X_SUBLANES
POOL_HIST = 2 * V7X_SUBLANES
ROW_TILE = 512
Q_TILE = 1024
KV_TILE = 2048
SUB_TILE_WIDE = 512
SUB_TILE_NARROW = 256
SUM_ROWS = 16
MAX_LAZY_RISE = 60.0
PAGES_PER_STEP = 8

F32 = jnp.float32
BF16 = jnp.bfloat16


def _resident(shape):
    nd = len(shape)
    return pl.BlockSpec(shape, lambda *_: (0,) * nd, pipeline_mode=pl.Buffered(1))


def _whole(shape):
    nd = len(shape)
    return pl.BlockSpec(shape, lambda *_: (0,) * nd)


def _params(*sem):
    return pltpu.CompilerParams(dimension_semantics=sem, vmem_limit_bytes=VMEM_LIMIT_BYTES)


def _rms(x, g):
    ms = jnp.mean(x * x, axis=-1, keepdims=True)
    return x * lax.rsqrt(ms + NORM_EPS) * g


def _dot(a, b):
    return jnp.dot(a, b, preferred_element_type=F32)


def _dot_nt(a, b):
    return lax.dot_general(a, b, (((1,), (1,)), ((), ())), preferred_element_type=F32)


def _mlp_residual(y, g_pre, g_post, w_up_ref, w_down_ref):
    h = _rms(y, g_pre).astype(BF16)
    r = jnp.maximum(_dot(h, w_up_ref[...]), 0.0)
    out = _dot((r * r).astype(BF16), w_down_ref[...])
    return y + _rms(out, g_post)


def _conv_pool_outputs(proj, conv_prev, pool_prev, pos, conv_w, pool_w_ref, pool_scale, w_out_ref):
    b_gate = proj[:, :D_CONV]
    conv_in = proj[:, D_CONV:2 * D_CONV] * proj[:, 2 * D_CONV:3 * D_CONV]
    u = proj[:, 3 * D_CONV:]
    conv_out = conv_w[CONV_WIDTH - 1:CONV_WIDTH] * conv_in
    for k in range(1, CONV_WIDTH):
        conv_out = conv_out + conv_w[CONV_WIDTH - 1 - k:CONV_WIDTH - k] * conv_prev(k)
    y_a = b_gate * conv_out
    ybs = []
    for g, w in enumerate(POOL_WINDOWS):
        sl = slice(g * POOL_GC, (g + 1) * POOL_GC)
        ug = u[:, sl]
        s = ug
        for j in range(1, w):
            s = s + pool_prev(j, sl)
        if pos is None:
            d = s / float(w) - ug
        else:
            d = s / jnp.minimum(pos + 1, w).astype(F32) - ug
        ybs.append(_dot(d.astype(BF16), pool_w_ref[g]))
    y_b = jnp.concatenate(ybs, axis=-1) * pool_scale
    mix = jnp.concatenate([y_a, y_b], axis=-1).astype(BF16)
    return _dot(mix, w_out_ref[...]), conv_in, u


def _mixer0_prompt_kernel(x_ref, g_pre_ref, g_post_ref, w_in_ref, conv_w_ref, pool_w_ref, pool_scale_ref,
                          w_out_ref, gm_pre_ref, gm_post_ref, w_up_ref, w_down_ref,
                          y_ref, conv_state_ref, pool_state_ref, conv_ext, pool_ext, *, tm):
    t = pl.program_id(1)

    @pl.when(t == 0)
    def _():
        conv_ext[0:CONV_HIST, :] = jnp.zeros((CONV_HIST, D_CONV), F32)
        pool_ext[0:POOL_HIST, :] = jnp.zeros((POOL_HIST, D_POOL), F32)

    x = x_ref[0]
    h = _rms(x, g_pre_ref[...]).astype(BF16)
    proj = _dot(h, w_in_ref[...])
    conv_ext[CONV_HIST:CONV_HIST + tm, :] = proj[:, D_CONV:2 * D_CONV] * proj[:, 2 * D_CONV:3 * D_CONV]
    pool_ext[POOL_HIST:POOL_HIST + tm, :] = proj[:, 3 * D_CONV:]
    pos = t * tm + lax.broadcasted_iota(jnp.int32, (tm, 1), 0)
    out, _, _ = _conv_pool_outputs(
        proj,
        lambda k: conv_ext[CONV_HIST - k:CONV_HIST - k + tm, :],
        lambda j, sl: pool_ext[POOL_HIST - j:POOL_HIST - j + tm, sl],
        pos, conv_w_ref[...], pool_w_ref, pool_scale_ref[...], w_out_ref)
    y1 = x + _rms(out, g_post_ref[...])
    y_ref[0] = _mlp_residual(y1, gm_pre_ref[...], gm_post_ref[...], w_up_ref, w_down_ref)

    @pl.when(t == pl.num_programs(1) - 1)
    def _():
        conv_state_ref[0] = conv_ext[CONV_HIST + tm - (CONV_WIDTH - 1):CONV_HIST + tm, :]
        pool_state_ref[0] = pool_ext[POOL_HIST + tm - POOL_BUF:POOL_HIST + tm, :]

    conv_ext[0:CONV_HIST, :] = conv_ext[tm:tm + CONV_HIST, :]
    pool_ext[0:POOL_HIST, :] = pool_ext[tm:tm + POOL_HIST, :]


def _mixer0_prompt(x, g_pre, g_post, w_in, conv_w, pool_w, pool_scale, w_out, gm_pre, gm_post, w_up, w_down):
    b, s, _ = x.shape
    tm = min(ROW_TILE, s)
    assert s % tm == 0 and tm >= POOL_HIST
    row = pl.BlockSpec((1, tm, D_MODEL), lambda i, t: (i, t, 0))
    return pl.pallas_call(
        functools.partial(_mixer0_prompt_kernel, tm=tm),
        grid=(b, s // tm),
        in_specs=[row, _resident(g_pre.shape), _resident(g_post.shape), _resident(w_in.shape),
                  _resident(conv_w.shape), _resident(pool_w.shape), _resident(pool_scale.shape),
                  _resident(w_out.shape), _resident(gm_pre.shape), _resident(gm_post.shape),
                  _resident(w_up.shape), _resident(w_down.shape)],
        out_specs=[row,
                   pl.BlockSpec((1, CONV_WIDTH - 1, D_CONV), lambda i, t: (i, 0, 0)),
                   pl.BlockSpec((1, POOL_BUF, D_POOL), lambda i, t: (i, 0, 0))],
        out_shape=[jax.ShapeDtypeStruct((b, s, D_MODEL), F32),
                   jax.ShapeDtypeStruct((b, CONV_WIDTH - 1, D_CONV), F32),
                   jax.ShapeDtypeStruct((b, POOL_BUF, D_POOL), F32)],
        scratch_shapes=[pltpu.VMEM((CONV_HIST + tm, D_CONV), F32), pltpu.VMEM((POOL_HIST + tm, D_POOL), F32)],
        compiler_params=_params("arbitrary", "arbitrary"),
        name="mixer0_prompt",
    )(x, g_pre, g_post, w_in, conv_w, pool_w, pool_scale, w_out, gm_pre, gm_post, w_up, w_down)


def _mixer0_sample_kernel(x_ref, conv_hist_ref, pool_hist_ref, g_pre_ref, g_post_ref, w_in_ref, conv_w_ref,
                          pool_w_ref, pool_scale_ref, w_out_ref, gm_pre_ref, gm_post_ref, w_up_ref, w_down_ref,
                          y_ref, conv_state_ref, pool_state_ref, *, full_windows):
    x = x_ref[...]
    h = _rms(x, g_pre_ref[...]).astype(BF16)
    proj = _dot(h, w_in_ref[...])
    assert full_windows
    out, conv_in, u = _conv_pool_outputs(
        proj,
        lambda k: conv_hist_ref[CONV_WIDTH - 1 - k],
        lambda j, sl: pool_hist_ref[POOL_BUF - j][:, sl],
        None, conv_w_ref[...], pool_w_ref, pool_scale_ref[...], w_out_ref)
    y1 = x + _rms(out, g_post_ref[...])
    y_ref[...] = _mlp_residual(y1, gm_pre_ref[...], gm_post_ref[...], w_up_ref, w_down_ref)
    for k in range(CONV_WIDTH - 2):
        conv_state_ref[k] = conv_hist_ref[k + 1]
    conv_state_ref[CONV_WIDTH - 2] = conv_in
    for j in range(POOL_BUF - 1):
        pool_state_ref[j] = pool_hist_ref[j + 1]
    pool_state_ref[POOL_BUF - 1] = u


def _mixer0_sample(x, conv_hist, pool_hist, past_len, *weights):
    n = x.shape[0]
    args = (x, conv_hist, pool_hist) + weights
    return pl.pallas_call(
        functools.partial(_mixer0_sample_kernel, full_windows=past_len + 1 >= max(POOL_WINDOWS)),
        grid=(1,),
        in_specs=[_resident(a.shape) for a in args],
        out_specs=[_whole((n, D_MODEL)), _whole(conv_hist.shape), _whole(pool_hist.shape)],
        out_shape=[jax.ShapeDtypeStruct((n, D_MODEL), F32),
                   jax.ShapeDtypeStruct(conv_hist.shape, F32),
                   jax.ShapeDtypeStruct(pool_hist.shape, F32)],
        compiler_params=_params("arbitrary"),
        name="mixer0_sample",
    )(*args)


def _inv_freq(idx):
    half = HEAD_DIM // 2
    x = (idx % half).astype(F32) * (1.0 / half)
    return jnp.exp(-x * jnp.log(jnp.full_like(x, ROPE_THETA)))


def _rope_lanes(x, cos, sin):
    lane = lax.broadcasted_iota(jnp.int32, (1, V7X_LANES), 1)
    first_half = (lane % HEAD_DIM) < HEAD_DIM // 2
    sin_signed = jnp.where(first_half, -sin, sin)
    outs = []
    for j in range(D_MODEL // V7X_LANES):
        xj = x[:, j * V7X_LANES:(j + 1) * V7X_LANES]
        partner = jnp.where(first_half,
                            pltpu.roll(xj, V7X_LANES - HEAD_DIM // 2, 1),
                            pltpu.roll(xj, HEAD_DIM // 2, 1))
        outs.append(xj * cos + partner * sin_signed)
    return jnp.concatenate(outs, axis=-1)


def _rope_rows(xt, cos, sin):
    half = HEAD_DIM // 2
    outs = []
    for g in range(D_MODEL // HEAD_DIM):
        x1 = xt[g * HEAD_DIM:g * HEAD_DIM + half]
        x2 = xt[g * HEAD_DIM + half:(g + 1) * HEAD_DIM]
        outs += [x1 * cos - x2 * sin, x2 * cos + x1 * sin]
    return jnp.concatenate(outs, axis=0)


def _qkv_prompt_kernel(y_ref, g_ref, wqt_ref, wkt_ref, wv_ref, qt_ref, kt_ref, kb_ref, v_ref, vtb_ref,
                       cos_t, sin_t, *, tm):
    t = pl.program_id(1)
    half = HEAD_DIM // 2
    freq = _inv_freq(lax.broadcasted_iota(jnp.int32, (half, 1), 0))

    @pl.when((pl.program_id(0) == 0) & (t == 0))
    def _():
        r = lax.broadcasted_iota(jnp.int32, (half, tm), 1).astype(F32)
        cos_t[...] = jnp.cos(r * freq)
        sin_t[...] = jnp.sin(r * freq)

    t0 = (t * tm).astype(F32)
    cb, sb = jnp.cos(t0 * freq), jnp.sin(t0 * freq)
    cos = cb * cos_t[...] - sb * sin_t[...]
    sin = sb * cos_t[...] + cb * sin_t[...]

    h = _rms(y_ref[0], g_ref[...]).astype(BF16)
    qt = _rope_rows(_dot_nt(wqt_ref[...], h), cos, sin)
    qt_ref[0] = (qt * (HEAD_DIM ** -0.5 * LOG2E)).astype(BF16)
    kt = _rope_rows(_dot_nt(wkt_ref[...], h), cos, sin)
    kt_ref[0] = kt
    kb_ref[0] = kt.T.astype(BF16)
    v = _dot(h, wv_ref[...])
    v_ref[0] = v
    vtb_ref[0] = v.T.astype(BF16)


def _qkv_prompt(y, g, w_q, w_k, w_v):
    b, s, _ = y.shape
    tm = min(ROW_TILE, s)
    assert s % tm == 0
    row = pl.BlockSpec((1, tm, D_MODEL), lambda i, t: (i, t, 0))
    col = pl.BlockSpec((1, D_MODEL, tm), lambda i, t: (i, 0, t))
    weights = (w_q.T, w_k.T, w_v)
    return pl.pallas_call(
        functools.partial(_qkv_prompt_kernel, tm=tm),
        grid=(b, s // tm),
        in_specs=[row, _resident(g.shape)] + [_resident(w.shape) for w in weights],
        out_specs=[col, col, row, row, col],
        out_shape=[jax.ShapeDtypeStruct((b, D_MODEL, s), BF16),
                   jax.ShapeDtypeStruct((b, D_MODEL, s), F32),
                   jax.ShapeDtypeStruct((b, s, D_MODEL), BF16),
                   jax.ShapeDtypeStruct((b, s, D_MODEL), F32),
                   jax.ShapeDtypeStruct((b, D_MODEL, s), BF16)],
        scratch_shapes=[pltpu.VMEM((HEAD_DIM // 2, tm), F32), pltpu.VMEM((HEAD_DIM // 2, tm), F32)],
        compiler_params=_params("arbitrary", "arbitrary"),
        name="qkv_rope_prompt",
    )(y, g, *weights)


def _qkv_sample_kernel(y_ref, g_ref, wq_ref, wk_ref, wv_ref, q_ref, k_ref, v_ref, *, pos):
    f_l = _inv_freq(lax.broadcasted_iota(jnp.int32, (1, V7X_LANES), 1))
    cos, sin = jnp.cos(float(pos) * f_l), jnp.sin(float(pos) * f_l)
    h = _rms(y_ref[...], g_ref[...]).astype(BF16)
    q = _rope_lanes(_dot(h, wq_ref[...]), cos, sin)
    q_ref[...] = (q * (HEAD_DIM ** -0.5 * LOG2E)).astype(BF16)
    k_ref[...] = _rope_lanes(_dot(h, wk_ref[...]), cos, sin)
    v_ref[...] = _dot(h, wv_ref[...])


def _qkv_sample(y, g, w_q, w_k, w_v, pos):
    n = y.shape[0]
    args = (y, g, w_q, w_k, w_v)
    return pl.pallas_call(
        functools.partial(_qkv_sample_kernel, pos=pos),
        grid=(1,),
        in_specs=[_resident(a.shape) for a in args],
        out_specs=[_whole((n, D_MODEL))] * 3,
        out_shape=[jax.ShapeDtypeStruct((n, D_MODEL), BF16),
                   jax.ShapeDtypeStruct((n, D_MODEL), F32),
                   jax.ShapeDtypeStruct((n, D_MODEL), F32)],
        compiler_params=_params("arbitrary"),
        name="qkv_rope_sample",
    )(*args)


def _diff_lambda(lam_p, lam_init):
    a = jnp.sum(lam_p[0:1] * lam_p[1:2], axis=-1, keepdims=True)
    b = jnp.sum(lam_p[2:3] * lam_p[3:4], axis=-1, keepdims=True)
    return jnp.exp(a) - jnp.exp(b) + lam_init


def _prompt_attn_kernel(qt_ref, k_ref, vt_ref, lam_ref, g_ref, o_ref, *, tq, tk, sub_wide, sub_narrow, lam_init):
    qi = pl.program_id(2)
    qt = qt_ref[0]
    row_map = lax.broadcasted_iota(jnp.int32, (V_DIM, tq), 0) // HEAD_DIM
    qz = tuple(jnp.where(row_map == c, qt, jnp.zeros_like(qt)) for c in range(2))

    def kv_step(start, width, sub, carry, masked, lazy):
        start = pl.multiple_of(start, tq)
        items = [(u, c) for u in range(width // sub) for c in range(2)]
        ref0 = [carry[0][0], carry[1][0]]
        state = [list(carry[0]), list(carry[1])]
        top = [carry[0][0], carry[1][0]]
        scores, probs = {}, {}

        def first_lane(u):
            return u * sub if masked else 0

        def update_lanes(old, off, new):
            return new if off == 0 else jnp.concatenate([old[:, :off], new], axis=1)

        def stage_scores(u, c):
            off = first_lane(u)
            kblk = k_ref[0, pl.ds(start + u * sub, sub), :]
            s = _dot(kblk, qz[c][:, off:])
            if masked:
                key = lax.broadcasted_iota(jnp.int32, s.shape, 0)
                qry = lax.broadcasted_iota(jnp.int32, s.shape, 1)
                s = jnp.where(key <= qry, s, NEG_INF)
            scores[(u, c)] = s

        def stage_exp(u, c):
            off = first_lane(u)
            s = scores.pop((u, c))
            smax = jnp.max(s, axis=0, keepdims=True)
            if lazy:
                top[c] = update_lanes(top[c], off, jnp.maximum(top[c][:, off:], smax))
                probs[(u, c)] = (None, jnp.exp2(s - ref0[c][:, off:]).astype(BF16))
            else:
                m = state[c][0]
                m_new = jnp.maximum(m[:, off:], smax)
                state[c][0] = update_lanes(m, off, m_new)
                probs[(u, c)] = (jnp.exp2(m[:, off:] - m_new), jnp.exp2(s - m_new).astype(BF16))

        def stage_values(u, c):
            off = first_lane(u)
            alpha, p = probs.pop((u, c))
            vtblk = jnp.concatenate([vt_ref[0, :, pl.ds(start + u * sub, sub)],
                                     jnp.ones((SUM_ROWS, sub), BF16)], axis=0)
            acc = state[c][1]
            old = acc[:, off:] if alpha is None else alpha * acc[:, off:]
            state[c][1] = update_lanes(acc, off, old + _dot(vtblk, p))

        for t in range(len(items) + 2):
            if t < len(items):
                stage_scores(*items[t])
            if 0 <= t - 1 < len(items):
                stage_exp(*items[t - 1])
            if 0 <= t - 2 < len(items):
                stage_values(*items[t - 2])
        if not lazy:
            return tuple(tuple(st) for st in state)
        rise = jnp.maximum(jnp.max(top[0] - ref0[0]), jnp.max(top[1] - ref0[1]))
        out = tuple((top[c], jnp.exp2(ref0[c] - top[c]) * state[c][1]) for c in range(2))
        return out, rise

    def guarded_step(start, width, sub, carry, masked):
        fast, rise = kv_step(start, width, sub, carry, masked, True)
        return lax.cond(rise <= MAX_LAZY_RISE, lambda: fast,
                        lambda: kv_step(start, width, sub, carry, masked, False))

    carry = tuple((_dot(k_ref[0, 0:V7X_SUBLANES, :], qz[c])[0:1], jnp.zeros((V_DIM + SUM_ROWS, tq), F32))
                  for c in range(2))
    n_wide = (qi * tq) // tk
    carry = lax.fori_loop(0, n_wide, lambda j, c: guarded_step(j * tk, tk, sub_wide, c, False), carry)
    if tk > tq:
        carry = lax.fori_loop(n_wide * (tk // tq), qi,
                              lambda j, c: guarded_step(j * tq, tq, sub_narrow, c, False), carry)
    (_, a1), (_, a2) = guarded_step(qi * tq, tq, sub_narrow, carry, True)
    lam = _diff_lambda(lam_ref[...], lam_init)
    ot = a1[:V_DIM] / a1[V_DIM:V_DIM + 1] - lam * (a2[:V_DIM] / a2[V_DIM:V_DIM + 1])
    ms = jnp.mean(ot * ot, axis=0, keepdims=True)
    ot = ot * lax.rsqrt(ms + NORM_EPS) * (g_ref[...] * (1.0 - lam_init))
    o_ref[0] = ot.T.astype(BF16)


def _prompt_attention(qt, kb, vt, lam_p, subln_col, lam_init, tq=Q_TILE, tk=KV_TILE,
                      sub_wide=SUB_TILE_WIDE, sub_narrow=SUB_TILE_NARROW):
    b, _, s = qt.shape
    tq = min(tq, s)
    tk = max(min(tk, s), tq)
    assert s % tk == 0 and tk % tq == 0
    return pl.pallas_call(
        functools.partial(_prompt_attn_kernel, tq=tq, tk=tk, sub_wide=min(sub_wide, tk),
                          sub_narrow=min(sub_narrow, tq), lam_init=lam_init),
        grid=(b, N_HEADS, s // tq),
        in_specs=[pl.BlockSpec((1, V_DIM, tq), lambda i, h, t: (i, h, t)),
                  pl.BlockSpec((1, s, V_DIM), lambda i, h, t: (i, 0, h)),
                  pl.BlockSpec((1, V_DIM, s), lambda i, h, t: (i, h, 0)),
                  pl.BlockSpec(lam_p.shape, lambda i, h, t: (0, 0)),
                  pl.BlockSpec(subln_col.shape, lambda i, h, t: (0, 0))],
        out_specs=pl.BlockSpec((1, tq, V_DIM), lambda i, h, t: (i, t, h)),
        out_shape=jax.ShapeDtypeStruct((b, s, D_MODEL), BF16),
        compiler_params=_params("arbitrary", "arbitrary", "arbitrary"),
        name="prompt_diff_attention",
    )(qt, kb, vt, lam_p, subln_col)


def _decode_attn_kernel(pt_ref, q_ref, ks_ref, vs_ref, *rest, n_pg, lam_init):
    kt_refs, v_refs = rest[:n_pg], rest[n_pg:2 * n_pg]
    lam_ref, g_ref, o_ref, qblk, m_s, l_s, acc_s = rest[2 * n_pg:]
    p = pl.program_id(1)

    @pl.when(p == 0)
    def _():
        rowi = lax.broadcasted_iota(jnp.int32, (N_MAPS, D_MODEL), 0)
        coli = lax.broadcasted_iota(jnp.int32, (N_MAPS, D_MODEL), 1)
        own_k = (coli // HEAD_DIM) == ((rowi % N_HEADS) * 2 + rowi // N_HEADS)
        qb = jnp.where(own_k, jnp.broadcast_to(q_ref[0].astype(F32), (N_MAPS, D_MODEL)), 0.0)
        qblk[...] = qb.astype(BF16)
        m_s[...] = jnp.sum(qb * ks_ref[0], axis=-1, keepdims=True)
        l_s[...] = jnp.ones((N_MAPS, 1), F32)
        acc_s[...] = jnp.broadcast_to(vs_ref[0], (N_MAPS, D_MODEL))

    s = jnp.concatenate([_dot(qblk[...], kt[0].astype(BF16)) for kt in kt_refs], axis=-1)
    m_new = jnp.maximum(m_s[...], jnp.max(s, axis=-1, keepdims=True))
    alpha = jnp.exp2(m_s[...] - m_new)
    pr = jnp.exp2(s - m_new)
    l_s[...] = alpha * l_s[...] + jnp.sum(pr, axis=-1, keepdims=True)
    m_s[...] = m_new
    pr = pr.astype(BF16)
    cols = []
    for h in range(N_HEADS):
        acc_h = None
        for g, v_ref in enumerate(v_refs):
            vh = v_ref[0, pl.ds(h, PAGE_SIZE, stride=N_HEADS), :].astype(BF16)
            part = _dot(pr[:, g * PAGE_SIZE:(g + 1) * PAGE_SIZE], vh)
            acc_h = part if acc_h is None else acc_h + part
        cols.append(acc_h)
    acc_s[...] = alpha * acc_s[...] + jnp.concatenate(cols, axis=-1)

    @pl.when(p == pl.num_programs(1) - 1)
    def _():
        o_all = acc_s[...] / l_s[...]
        lam = _diff_lambda(lam_ref[...], lam_init)
        diff = o_all[:N_HEADS] - lam * o_all[N_HEADS:]
        own_v = (lax.broadcasted_iota(jnp.int32, (N_HEADS, D_MODEL), 1) // V_DIM
                 == lax.broadcasted_iota(jnp.int32, (N_HEADS, D_MODEL), 0))
        diff = jnp.where(own_v, diff, 0.0)
        ms = jnp.sum(diff * diff, axis=-1, keepdims=True) * (1.0 / V_DIM)
        normed = diff * lax.rsqrt(ms + NORM_EPS)
        o = jnp.sum(normed, axis=0, keepdims=True) * g_ref[...] * (1.0 - lam_init)
        o_ref[0] = o.astype(BF16)


def _decode_attention(q, k_self, v_self, cache_kt, cache_v2d, page_table, lam_p, subln_row, lam_init):
    n, n_pages = page_table.shape
    n_pg = math.gcd(PAGES_PER_STEP, n_pages)

    def page_spec(block, g):
        return pl.BlockSpec(block, lambda i, p, pt: (pt[i * n_pages + p * n_pg + g], 0, 0))

    one = pl.BlockSpec((1, 1, D_MODEL), lambda i, p, pt: (i, 0, 0))
    return pl.pallas_call(
        functools.partial(_decode_attn_kernel, n_pg=n_pg, lam_init=lam_init),
        grid_spec=pltpu.PrefetchScalarGridSpec(
            num_scalar_prefetch=1,
            grid=(n, n_pages // n_pg),
            in_specs=[one, one, one]
                     + [page_spec((1, D_MODEL, PAGE_SIZE), g) for g in range(n_pg)]
                     + [page_spec((1, PAGE_SIZE * N_HEADS, V_DIM), g) for g in range(n_pg)]
                     + [pl.BlockSpec(lam_p.shape, lambda i, p, pt: (0, 0)),
                        pl.BlockSpec(subln_row.shape, lambda i, p, pt: (0, 0))],
            out_specs=one,
            scratch_shapes=[pltpu.VMEM((N_MAPS, D_MODEL), BF16), pltpu.VMEM((N_MAPS, 1), F32),
                            pltpu.VMEM((N_MAPS, 1), F32), pltpu.VMEM((N_MAPS, D_MODEL), F32)]),
        out_shape=jax.ShapeDtypeStruct((n, 1, D_MODEL), BF16),
        compiler_params=_params("arbitrary", "arbitrary"),
        name="decode_diff_attention",
    )(page_table.reshape(-1), q, k_self, v_self, *([cache_kt] * n_pg), *([cache_v2d] * n_pg), lam_p, subln_row)


def _attn_out_kernel(y_ref, o_ref, w_o_ref, g_post_ref, gm_pre_ref, gm_post_ref, w_up_ref, w_down_ref, out_ref):
    y1 = y_ref[...] + _rms(_dot(o_ref[...], w_o_ref[...]), g_post_ref[...])
    out_ref[...] = _mlp_residual(y1, gm_pre_ref[...], gm_post_ref[...], w_up_ref, w_down_ref)


def _attn_out(y2d, o2d, w_o, g_post, gm_pre, gm_post, w_up, w_down):
    m = y2d.shape[0]
    tm = min(ROW_TILE, m)
    assert m % tm == 0
    row = pl.BlockSpec((tm, D_MODEL), lambda i: (i, 0))
    weights = (w_o, g_post, gm_pre, gm_post, w_up, w_down)
    return pl.pallas_call(
        _attn_out_kernel,
        grid=(m // tm,),
        in_specs=[row, row] + [_resident(a.shape) for a in weights],
        out_specs=row,
        out_shape=jax.ShapeDtypeStruct((m, D_MODEL), F32),
        compiler_params=_params("arbitrary"),
        name="attn_out_mlp",
    )(y2d, o2d, *weights)


def kernel(x_prompt, x_sample, state_conv, state_pool, cache_k, cache_v, page_table, norm_mix_pre, norm_mix_post, norm_mlp_pre, norm_mlp_post, mlp_up, mlp_down, sc_w_in, sc_conv_w, pool_w, pool_scale, sc_w_out, attn_w_qkv, attn_lambda, attn_subln, attn_w_o):
    b, s, _ = x_prompt.shape
    n, t_dec, _ = x_sample.shape
    assert t_dec == 1 and norm_mix_pre.shape[0] == 2
    past_len = page_table.shape[1] * PAGE_SIZE
    lam_init = 0.8 - 0.6 * math.exp(-0.3 * 1)

    def gain(a, layer):
        return a[layer].reshape(1, D_MODEL)

    bf = lambda a: a.astype(BF16)
    l0 = (gain(norm_mix_pre, 0), gain(norm_mix_post, 0), bf(sc_w_in[0]), sc_conv_w[0], bf(pool_w[0]),
          pool_scale[0].reshape(1, D_POOL), bf(sc_w_out[0]), gain(norm_mlp_pre, 0), gain(norm_mlp_post, 0),
          bf(mlp_up[0]), bf(mlp_down[0]))
    l1_out = (bf(attn_w_o[0]), gain(norm_mix_post, 1), gain(norm_mlp_pre, 1), gain(norm_mlp_post, 1),
              bf(mlp_up[1]), bf(mlp_down[1]))
    w_qkv = bf(attn_w_qkv[0])
    w_q, w_k, w_v = w_qkv[:, :D_MODEL], w_qkv[:, D_MODEL:2 * D_MODEL], w_qkv[:, 2 * D_MODEL:]
    g_qkv = gain(norm_mix_pre, 1)
    lam_p = attn_lambda[0]
    subln = attn_subln[0].reshape(1, V_DIM)

    yp, conv_p, pool_p = _mixer0_prompt(x_prompt, *l0)
    ys, conv_s, pool_s = _mixer0_sample(x_sample.reshape(n, D_MODEL),
                                        jnp.swapaxes(state_conv[0], 0, 1), jnp.swapaxes(state_pool[0], 0, 1),
                                        past_len, *l0)

    qtp, ktp, kbp, vp, vtbp = _qkv_prompt(yp, g_qkv, w_q, w_k, w_v)
    qs, ks, vs = _qkv_sample(ys, g_qkv, w_q, w_k, w_v, past_len)
    op = _prompt_attention(qtp, kbp, vtbp, lam_p, subln.reshape(V_DIM, 1), lam_init)
    n_pool = cache_k.shape[1]
    cache_kt = jnp.transpose(cache_k[0], (0, 2, 3, 4, 1)).reshape(n_pool, D_MODEL, PAGE_SIZE)
    cache_v2d = cache_v[0].reshape(n_pool, PAGE_SIZE * N_HEADS, V_DIM)
    os_ = _decode_attention(qs.reshape(n, 1, D_MODEL), ks.reshape(n, 1, D_MODEL), vs.reshape(n, 1, D_MODEL),
                            cache_kt, cache_v2d, page_table, lam_p, jnp.tile(subln, (1, N_HEADS)), lam_init)
    yp = _attn_out(yp.reshape(b * s, D_MODEL), op.reshape(b * s, D_MODEL), *l1_out)
    ys = _attn_out(ys, os_.reshape(n, D_MODEL), *l1_out)

    k_rows_p = jnp.transpose(ktp.reshape(b, N_HEADS, 2, HEAD_DIM, s), (0, 4, 1, 2, 3))
    return (yp.reshape(b, s, D_MODEL), ys.reshape(n, 1, D_MODEL),
            conv_p[None], jnp.swapaxes(conv_s, 0, 1)[None],
            pool_p[None], jnp.swapaxes(pool_s, 0, 1)[None],
            k_rows_p[None], vp.reshape(1, b, s, N_HEADS, V_DIM),
            ks.reshape(1, n, 1, N_HEADS, 2, HEAD_DIM), vs.reshape(1, n, 1, N_HEADS, V_DIM))
```

```python
import functools
import math

import jax
import jax.numpy as jnp
from jax import lax
from jax.experimental import pallas as pl
from jax.experimental.pallas import tpu as pltpu

D_MODEL = 1024
D_CONV = D_MODEL // 2
D_POOL = D_MODEL // 2
CONV_WIDTH = 3
POOL_WINDOWS = (2, 4, 8, 16)
POOL_GC = D_POOL // len(POOL_WINDOWS)
POOL_BUF = max(POOL_WINDOWS) - 1
HEAD_DIM = 64
N_HEADS = D_MODEL // (2 * HEAD_DIM)
N_MAPS = 2 * N_HEADS
V_DIM = 2 * HEAD_DIM
D_FF = 4 * D_MODEL
ROPE_THETA = 10000.0
NORM_EPS = 1e-6
PAGE_SIZE = 128
NEG_INF = -1e30
LOG2E = math.log2(math.e)

V7X_LANES = 128
V7X_SUBLANES = 8
V7X_VMEM_BYTES = 64 * 1024 * 1024
VMEM_LIMIT_BYTES = V7X_VMEM_BYTES - 8 * 1024 * 1024

CONV_HIST = V7X_SUBLANES
POOL_HIST = 2 * V7X_SUBLANES
ROW_TILE = 512
Q_TILE = 2048
KV_TILE = 2048
SUB_TILE_WIDE = 512
SUB_TILE_NARROW = 256
SUM_ROWS = 16
MAX_LAZY_RISE = 60.0
PAGES_PER_STEP = 8

F32 = jnp.float32
BF16 = jnp.bfloat16


def _resident(shape):
    nd = len(shape)
    return pl.BlockSpec(shape, lambda *_: (0,) * nd, pipeline_mode=pl.Buffered(1))


def _whole(shape):
    nd = len(shape)
    return pl.BlockSpec(shape, lambda *_: (0,) * nd)


def _params(*sem):
    return pltpu.CompilerParams(dimension_semantics=sem, vmem_limit_bytes=VMEM_LIMIT_BYTES)


def _rms(x, g):
    ms = jnp.mean(x * x, axis=-1, keepdims=True)
    return x * lax.rsqrt(ms + NORM_EPS) * g


def _dot(a, b):
    return jnp.dot(a, b, preferred_element_type=F32)


def _dot_nt(a, b):
    return lax.dot_general(a, b, (((1,), (1,)), ((), ())), preferred_element_type=F32)


def _mlp_residual(y, g_pre, g_post, w_up_ref, w_down_ref):
    h = _rms(y, g_pre).astype(BF16)
    r = jnp.maximum(_dot(h, w_up_ref[...]), 0.0)
    out = _dot((r * r).astype(BF16), w_down_ref[...])
    return y + _rms(out, g_post)


def _conv_pool_outputs(proj, conv_prev, pool_prev, pos, conv_w, pool_w_ref, pool_scale, w_out_ref):
    b_gate = proj[:, :D_CONV]
    conv_in = proj[:, D_CONV:2 * D_CONV] * proj[:, 2 * D_CONV:3 * D_CONV]
    u = proj[:, 3 * D_CONV:]
    conv_out = conv_w[CONV_WIDTH - 1:CONV_WIDTH] * conv_in
    for k in range(1, CONV_WIDTH):
        conv_out = conv_out + conv_w[CONV_WIDTH - 1 - k:CONV_WIDTH - k] * conv_prev(k)
    y_a = b_gate * conv_out
    ybs = []
    for g, w in enumerate(POOL_WINDOWS):
        sl = slice(g * POOL_GC, (g + 1) * POOL_GC)
        ug = u[:, sl]
        s = ug
        for j in range(1, w):
            s = s + pool_prev(j, sl)
        if pos is None:
            d = s / float(w) - ug
        else:
            d = s / jnp.minimum(pos + 1, w).astype(F32) - ug
        ybs.append(_dot(d.astype(BF16), pool_w_ref[g]))
    y_b = jnp.concatenate(ybs, axis=-1) * pool_scale
    mix = jnp.concatenate([y_a, y_b], axis=-1).astype(BF16)
    return _dot(mix, w_out_ref[...]), conv_in, u


def _mixer0_prompt_kernel(x_ref, g_pre_ref, g_post_ref, w_in_ref, conv_w_ref, pool_w_ref, pool_scale_ref,
                          w_out_ref, gm_pre_ref, gm_post_ref, w_up_ref, w_down_ref,
                          y_ref, conv_state_ref, pool_state_ref, conv_ext, pool_ext, *, tm):
    t = pl.program_id(1)

    @pl.when(t == 0)
    def _():
        conv_ext[0:CONV_HIST, :] = jnp.zeros((CONV_HIST, D_CONV), F32)
        pool_ext[0:POOL_HIST, :] = jnp.zeros((POOL_HIST, D_POOL), F32)

    x = x_ref[0]
    h = _rms(x, g_pre_ref[...]).astype(BF16)
    proj = _dot(h, w_in_ref[...])
    conv_ext[CONV_HIST:CONV_HIST + tm, :] = proj[:, D_CONV:2 * D_CONV] * proj[:, 2 * D_CONV:3 * D_CONV]
    pool_ext[POOL_HIST:POOL_HIST + tm, :] = proj[:, 3 * D_CONV:]
    pos = t * tm + lax.broadcasted_iota(jnp.int32, (tm, 1), 0)
    out, _, _ = _conv_pool_outputs(
        proj,
        lambda k: conv_ext[CONV_HIST - k:CONV_HIST - k + tm, :],
        lambda j, sl: pool_ext[POOL_HIST - j:POOL_HIST - j + tm, sl],
        pos, conv_w_ref[...], pool_w_ref, pool_scale_ref[...], w_out_ref)
    y1 = x + _rms(out, g_post_ref[...])
    y_ref[0] = _mlp_residual(y1, gm_pre_ref[...], gm_post_ref[...], w_up_ref, w_down_ref)

    @pl.when(t == pl.num_programs(1) - 1)
    def _():
        conv_state_ref[0] = conv_ext[CONV_HIST + tm - (CONV_WIDTH - 1):CONV_HIST + tm, :]
        pool_state_ref[0] = pool_ext[POOL_HIST + tm - POOL_BUF:POOL_HIST + tm, :]

    conv_ext[0:CONV_HIST, :] = conv_ext[tm:tm + CONV_HIST, :]
    pool_ext[0:POOL_HIST, :] = pool_ext[tm:tm + POOL_HIST, :]


def _mixer0_prompt(x, g_pre, g_post, w_in, conv_w, pool_w, pool_scale, w_out, gm_pre, gm_post, w_up, w_down):
    b, s, _ = x.shape
    tm = min(ROW_TILE, s)
    assert s % tm == 0 and tm >= POOL_HIST
    row = pl.BlockSpec((1, tm, D_MODEL), lambda i, t: (i, t, 0))
    return pl.pallas_call(
        functools.partial(_mixer0_prompt_kernel, tm=tm),
        grid=(b, s // tm),
        in_specs=[row, _resident(g_pre.shape), _resident(g_post.shape), _resident(w_in.shape),
                  _resident(conv_w.shape), _resident(pool_w.shape), _resident(pool_scale.shape),
                  _resident(w_out.shape), _resident(gm_pre.shape), _resident(gm_post.shape),
                  _resident(w_up.shape), _resident(w_down.shape)],
        out_specs=[row,
                   pl.BlockSpec((1, CONV_WIDTH - 1, D_CONV), lambda i, t: (i, 0, 0)),
                   pl.BlockSpec((1, POOL_BUF, D_POOL), lambda i, t: (i, 0, 0))],
        out_shape=[jax.ShapeDtypeStruct((b, s, D_MODEL), F32),
                   jax.ShapeDtypeStruct((b, CONV_WIDTH - 1, D_CONV), F32),
                   jax.ShapeDtypeStruct((b, POOL_BUF, D_POOL), F32)],
        scratch_shapes=[pltpu.VMEM((CONV_HIST + tm, D_CONV), F32), pltpu.VMEM((POOL_HIST + tm, D_POOL), F32)],
        compiler_params=_params("arbitrary", "arbitrary"),
        name="mixer0_prompt",
    )(x, g_pre, g_post, w_in, conv_w, pool_w, pool_scale, w_out, gm_pre, gm_post, w_up, w_down)


def _mixer0_sample_kernel(x_ref, conv_hist_ref, pool_hist_ref, g_pre_ref, g_post_ref, w_in_ref, conv_w_ref,
                          pool_w_ref, pool_scale_ref, w_out_ref, gm_pre_ref, gm_post_ref, w_up_ref, w_down_ref,
                          y_ref, conv_state_ref, pool_state_ref, *, full_windows):
    x = x_ref[...]
    h = _rms(x, g_pre_ref[...]).astype(BF16)
    proj = _dot(h, w_in_ref[...])
    assert full_windows
    out, conv_in, u = _conv_pool_outputs(
        proj,
        lambda k: conv_hist_ref[CONV_WIDTH - 1 - k],
        lambda j, sl: pool_hist_ref[POOL_BUF - j][:, sl],
        None, conv_w_ref[...], pool_w_ref, pool_scale_ref[...], w_out_ref)
    y1 = x + _rms(out, g_post_ref[...])
    y_ref[...] = _mlp_residual(y1, gm_pre_ref[...], gm_post_ref[...], w_up_ref, w_down_ref)
    for k in range(CONV_WIDTH - 2):
        conv_state_ref[k] = conv_hist_ref[k + 1]
    conv_state_ref[CONV_WIDTH - 2] = conv_in
    for j in range(POOL_BUF - 1):
        pool_state_ref[j] = pool_hist_ref[j + 1]
    pool_state_ref[POOL_BUF - 1] = u


def _mixer0_sample(x, conv_hist, pool_hist, past_len, *weights):
    n = x.shape[0]
    args = (x, conv_hist, pool_hist) + weights
    return pl.pallas_call(
        functools.partial(_mixer0_sample_kernel, full_windows=past_len + 1 >= max(POOL_WINDOWS)),
        grid=(1,),
        in_specs=[_resident(a.shape) for a in args],
        out_specs=[_whole((n, D_MODEL)), _whole(conv_hist.shape), _whole(pool_hist.shape)],
        out_shape=[jax.ShapeDtypeStruct((n, D_MODEL), F32),
                   jax.ShapeDtypeStruct(conv_hist.shape, F32),
                   jax.ShapeDtypeStruct(pool_hist.shape, F32)],
        compiler_params=_params("arbitrary"),
        name="mixer0_sample",
    )(*args)


def _inv_freq(idx):
    half = HEAD_DIM // 2
    x = (idx % half).astype(F32) * (1.0 / half)
    return jnp.exp(-x * jnp.log(jnp.full_like(x, ROPE_THETA)))


def _rope_lanes(x, cos, sin):
    lane = lax.broadcasted_iota(jnp.int32, (1, V7X_LANES), 1)
    first_half = (lane % HEAD_DIM) < HEAD_DIM // 2
    sin_signed = jnp.where(first_half, -sin, sin)
    outs = []
    for j in range(D_MODEL // V7X_LANES):
        xj = x[:, j * V7X_LANES:(j + 1) * V7X_LANES]
        partner = jnp.where(first_half,
                            pltpu.roll(xj, V7X_LANES - HEAD_DIM // 2, 1),
                            pltpu.roll(xj, HEAD_DIM // 2, 1))
        outs.append(xj * cos + partner * sin_signed)
    return jnp.concatenate(outs, axis=-1)


def _rope_rows(xt, cos, sin):
    half = HEAD_DIM // 2
    outs = []
    for g in range(D_MODEL // HEAD_DIM):
        x1 = xt[g * HEAD_DIM:g * HEAD_DIM + half]
        x2 = xt[g * HEAD_DIM + half:(g + 1) * HEAD_DIM]
        outs += [x1 * cos - x2 * sin, x2 * cos + x1 * sin]
    return jnp.concatenate(outs, axis=0)


def _qkv_prompt_kernel(y_ref, g_ref, wqt_ref, wkt_ref, wv_ref, qt_ref, kt_ref, kb_ref, v_ref, vtb_ref,
                       cos_t, sin_t, *, tm):
    t = pl.program_id(1)
    half = HEAD_DIM // 2
    freq = _inv_freq(lax.broadcasted_iota(jnp.int32, (half, 1), 0))

    @pl.when((pl.program_id(0) == 0) & (t == 0))
    def _():
        r = lax.broadcasted_iota(jnp.int32, (half, tm), 1).astype(F32)
        cos_t[...] = jnp.cos(r * freq)
        sin_t[...] = jnp.sin(r * freq)

    t0 = (t * tm).astype(F32)
    cb, sb = jnp.cos(t0 * freq), jnp.sin(t0 * freq)
    cos = cb * cos_t[...] - sb * sin_t[...]
    sin = sb * cos_t[...] + cb * sin_t[...]

    h = _rms(y_ref[0], g_ref[...]).astype(BF16)
    qt = _rope_rows(_dot_nt(wqt_ref[...], h), cos, sin)
    qt_ref[0] = (qt * (HEAD_DIM ** -0.5 * LOG2E)).astype(BF16)
    kt = _rope_rows(_dot_nt(wkt_ref[...], h), cos, sin)
    kt_ref[0] = kt
    kb_ref[0] = kt.T.astype(BF16)
    v = _dot(h, wv_ref[...])
    v_ref[0] = v
    vtb_ref[0] = v.T.astype(BF16)


def _qkv_prompt(y, g, w_q, w_k, w_v):
    b, s, _ = y.shape
    tm = min(ROW_TILE, s)
    assert s % tm == 0
    row = pl.BlockSpec((1, tm, D_MODEL), lambda i, t: (i, t, 0))
    col = pl.BlockSpec((1, D_MODEL, tm), lambda i, t: (i, 0, t))
    weights = (w_q.T, w_k.T, w_v)
    return pl.pallas_call(
        functools.partial(_qkv_prompt_kernel, tm=tm),
        grid=(b, s // tm),
        in_specs=[row, _resident(g.shape)] + [_resident(w.shape) for w in weights],
        out_specs=[col, col, row, row, col],
        out_shape=[jax.ShapeDtypeStruct((b, D_MODEL, s), BF16),
                   jax.ShapeDtypeStruct((b, D_MODEL, s), F32),
                   jax.ShapeDtypeStruct((b, s, D_MODEL), BF16),
                   jax.ShapeDtypeStruct((b, s, D_MODEL), F32),
                   jax.ShapeDtypeStruct((b, D_MODEL, s), BF16)],
        scratch_shapes=[pltpu.VMEM((HEAD_DIM // 2, tm), F32), pltpu.VMEM((HEAD_DIM // 2, tm), F32)],
        compiler_params=_params("arbitrary", "arbitrary"),
        name="qkv_rope_prompt",
    )(y, g, *weights)


def _qkv_sample_kernel(y_ref, g_ref, wq_ref, wk_ref, wv_ref, q_ref, k_ref, v_ref, *, pos):
    f_l = _inv_freq(lax.broadcasted_iota(jnp.int32, (1, V7X_LANES), 1))
    cos, sin = jnp.cos(float(pos) * f_l), jnp.sin(float(pos) * f_l)
    h = _rms(y_ref[...], g_ref[...]).astype(BF16)
    q = _rope_lanes(_dot(h, wq_ref[...]), cos, sin)
    q_ref[...] = (q * (HEAD_DIM ** -0.5 * LOG2E)).astype(BF16)
    k_ref[...] = _rope_lanes(_dot(h, wk_ref[...]), cos, sin)
    v_ref[...] = _dot(h, wv_ref[...])


def _qkv_sample(y, g, w_q, w_k, w_v, pos):
    n = y.shape[0]
    args = (y, g, w_q, w_k, w_v)
    return pl.pallas_call(
        functools.partial(_qkv_sample_kernel, pos=pos),
        grid=(1,),
        in_specs=[_resident(a.shape) for a in args],
        out_specs=[_whole((n, D_MODEL))] * 3,
        out_shape=[jax.ShapeDtypeStruct((n, D_MODEL), BF16),
                   jax.ShapeDtypeStruct((n, D_MODEL), F32),
                   jax.ShapeDtypeStruct((n, D_MODEL), F32)],
        compiler_params=_params("arbitrary"),
        name="qkv_rope_sample",
    )(*args)


def _diff_lambda(lam_p, lam_init):
    a = jnp.sum(lam_p[0:1] * lam_p[1:2], axis=-1, keepdims=True)
    b = jnp.sum(lam_p[2:3] * lam_p[3:4], axis=-1, keepdims=True)
    return jnp.exp(a) - jnp.exp(b) + lam_init


def _prompt_attn_kernel(qt_ref, k_ref, vt_ref, lam_ref, g_ref, o_ref, *, tq, tk, sub_wide, sub_narrow, lam_init):
    qi = pl.program_id(2)
    qt = qt_ref[0]
    row_map = lax.broadcasted_iota(jnp.int32, (V_DIM, tq), 0) // HEAD_DIM
    qz = tuple(jnp.where(row_map == c, qt, jnp.zeros_like(qt)) for c in range(2))

    def kv_step(start, width, sub, carry, masked, lazy):
        start = pl.multiple_of(start, tq)
        items = [(u, c) for u in range(width // sub) for c in range(2)]
        ref0 = [carry[0][0], carry[1][0]]
        state = [list(carry[0]), list(carry[1])]
        top = [carry[0][0], carry[1][0]]
        scores, probs = {}, {}

        def first_lane(u):
            return u * sub if masked else 0

        def update_lanes(old, off, new):
            return new if off == 0 else jnp.concatenate([old[:, :off], new], axis=1)

        def stage_scores(u, c):
            off = first_lane(u)
            kblk = k_ref[0, pl.ds(start + u * sub, sub), :]
            s = _dot(kblk, qz[c][:, off:])
            if masked:
                key = lax.broadcasted_iota(jnp.int32, s.shape, 0)
                qry = lax.broadcasted_iota(jnp.int32, s.shape, 1)
                s = jnp.where(key <= qry, s, NEG_INF)
            scores[(u, c)] = s

        def stage_exp(u, c):
            off = first_lane(u)
            s = scores.pop((u, c))
            smax = jnp.max(s, axis=0, keepdims=True)
            if lazy:
                top[c] = update_lanes(top[c], off, jnp.maximum(top[c][:, off:], smax))
                probs[(u, c)] = (None, jnp.exp2(s - ref0[c][:, off:]).astype(BF16))
            else:
                m = state[c][0]
                m_new = jnp.maximum(m[:, off:], smax)
                state[c][0] = update_lanes(m, off, m_new)
                probs[(u, c)] = (jnp.exp2(m[:, off:] - m_new), jnp.exp2(s - m_new).astype(BF16))

        def stage_values(u, c):
            off = first_lane(u)
            alpha, p = probs.pop((u, c))
            vtblk = jnp.concatenate([vt_ref[0, :, pl.ds(start + u * sub, sub)],
                                     jnp.ones((SUM_ROWS, sub), BF16)], axis=0)
            acc = state[c][1]
            old = acc[:, off:] if alpha is None else alpha * acc[:, off:]
            state[c][1] = update_lanes(acc, off, old + _dot(vtblk, p))

        for t in range(len(items) + 2):
            if t < len(items):
                stage_scores(*items[t])
            if 0 <= t - 1 < len(items):
                stage_exp(*items[t - 1])
            if 0 <= t - 2 < len(items):
                stage_values(*items[t - 2])
        if not lazy:
            return tuple(tuple(st) for st in state)
        rise = jnp.maximum(carry[2], jnp.maximum(top[0] - ref0[0], top[1] - ref0[1]))
        return tuple((top[c], jnp.exp2(ref0[c] - top[c]) * state[c][1]) for c in range(2)) + (rise,)

    def run_tile(lazy):
        carry = tuple((_dot(k_ref[0, 0:V7X_SUBLANES, :], qz[c])[0:1], jnp.zeros((V_DIM + SUM_ROWS, tq), F32))
                      for c in range(2))
        if lazy:
            carry += (jnp.zeros((1, tq), F32),)
        n_wide = (qi * tq) // tk
        carry = lax.fori_loop(0, n_wide, lambda j, c: kv_step(j * tk, tk, sub_wide, c, False, lazy), carry)
        if tk > tq:
            carry = lax.fori_loop(n_wide * (tk // tq), qi,
                                  lambda j, c: kv_step(j * tq, tq, sub_narrow, c, False, lazy), carry)
        return kv_step(qi * tq, tq, sub_narrow, carry, True, lazy)

    fast = run_tile(True)
    (_, a1), (_, a2) = lax.cond(jnp.max(fast[2]) <= MAX_LAZY_RISE, lambda: fast[:2], lambda: run_tile(False))
    lam = _diff_lambda(lam_ref[...], lam_init)
    ot = a1[:V_DIM] / a1[V_DIM:V_DIM + 1] - lam * (a2[:V_DIM] / a2[V_DIM:V_DIM + 1])
    ms = jnp.mean(ot * ot, axis=0, keepdims=True)
    ot = ot * lax.rsqrt(ms + NORM_EPS) * (g_ref[...] * (1.0 - lam_init))
    o_ref[0] = ot.T.astype(BF16)


def _prompt_attention(qt, kb, vt, lam_p, subln_col, lam_init, tq=Q_TILE, tk=KV_TILE,
                      sub_wide=SUB_TILE_WIDE, sub_narrow=SUB_TILE_NARROW):
    b, _, s = qt.shape
    tq = min(tq, s)
    tk = max(min(tk, s), tq)
    assert s % tk == 0 and tk % tq == 0
    return pl.pallas_call(
        functools.partial(_prompt_attn_kernel, tq=tq, tk=tk, sub_wide=min(sub_wide, tk),
                          sub_narrow=min(sub_narrow, tq), lam_init=lam_init),
        grid=(b, N_HEADS, s // tq),
        in_specs=[pl.BlockSpec((1, V_DIM, tq), lambda i, h, t: (i, h, t)),
                  pl.BlockSpec((1, s, V_DIM), lambda i, h, t: (i, 0, h)),
                  pl.BlockSpec((1, V_DIM, s), lambda i, h, t: (i, h, 0)),
                  pl.BlockSpec(lam_p.shape, lambda i, h, t: (0, 0)),
                  pl.BlockSpec(subln_col.shape, lambda i, h, t: (0, 0))],
        out_specs=pl.BlockSpec((1, tq, V_DIM), lambda i, h, t: (i, t, h)),
        out_shape=jax.ShapeDtypeStruct((b, s, D_MODEL), BF16),
        compiler_params=_params("arbitrary", "arbitrary", "arbitrary"),
        name="prompt_diff_attention",
    )(qt, kb, vt, lam_p, subln_col)


def _decode_attn_kernel(pt_ref, q_ref, ks_ref, vs_ref, *rest, n_pg, lam_init):
    kt_refs, v_refs = rest[:n_pg], rest[n_pg:2 * n_pg]
    lam_ref, g_ref, o_ref, qblk, m_s, l_s, acc_s = rest[2 * n_pg:]
    p = pl.program_id(1)

    @pl.when(p == 0)
    def _():
        rowi = lax.broadcasted_iota(jnp.int32, (N_MAPS, D_MODEL), 0)
        coli = lax.broadcasted_iota(jnp.int32, (N_MAPS, D_MODEL), 1)
        own_k = (coli // HEAD_DIM) == ((rowi % N_HEADS) * 2 + rowi // N_HEADS)
        qb = jnp.where(own_k, jnp.broadcast_to(q_ref[0].astype(F32), (N_MAPS, D_MODEL)), 0.0)
        qblk[...] = qb.astype(BF16)
        m_s[...] = jnp.sum(qb * ks_ref[0], axis=-1, keepdims=True)
        l_s[...] = jnp.ones((N_MAPS, 1), F32)
        acc_s[...] = jnp.broadcast_to(vs_ref[0], (N_MAPS, D_MODEL))

    s = jnp.concatenate([_dot(qblk[...], kt[0].astype(BF16)) for kt in kt_refs], axis=-1)
    m_new = jnp.maximum(m_s[...], jnp.max(s, axis=-1, keepdims=True))
    alpha = jnp.exp2(m_s[...] - m_new)
    pr = jnp.exp2(s - m_new)
    l_s[...] = alpha * l_s[...] + jnp.sum(pr, axis=-1, keepdims=True)
    m_s[...] = m_new
    pr = pr.astype(BF16)
    cols = []
    for h in range(N_HEADS):
        acc_h = None
        for g, v_ref in enumerate(v_refs):
            vh = v_ref[0, pl.ds(h, PAGE_SIZE, stride=N_HEADS), :].astype(BF16)
            part = _dot(pr[:, g * PAGE_SIZE:(g + 1) * PAGE_SIZE], vh)
            acc_h = part if acc_h is None else acc_h + part
        cols.append(acc_h)
    acc_s[...] = alpha * acc_s[...] + jnp.concatenate(cols, axis=-1)

    @pl.when(p == pl.num_programs(1) - 1)
    def _():
        o_all = acc_s[...] / l_s[...]
        lam = _diff_lambda(lam_ref[...], lam_init)
        diff = o_all[:N_HEADS] - lam * o_all[N_HEADS:]
        own_v = (lax.broadcasted_iota(jnp.int32, (N_HEADS, D_MODEL), 1) // V_DIM
                 == lax.broadcasted_iota(jnp.int32, (N_HEADS, D_MODEL), 0))
        diff = jnp.where(own_v, diff, 0.0)
        ms = jnp.sum(diff * diff, axis=-1, keepdims=True) * (1.0 / V_DIM)
        normed = diff * lax.rsqrt(ms + NORM_EPS)
        o = jnp.sum(normed, axis=0, keepdims=True) * g_ref[...] * (1.0 - lam_init)
        o_ref[0] = o.astype(BF16)


def _decode_attention(q, k_self, v_self, cache_kt, cache_v2d, page_table, lam_p, subln_row, lam_init):
    n, n_pages = page_table.shape
    n_pg = math.gcd(PAGES_PER_STEP, n_pages)

    def page_spec(block, g):
        return pl.BlockSpec(block, lambda i, p, pt: (pt[i * n_pages + p * n_pg + g], 0, 0))

    one = pl.BlockSpec((1, 1, D_MODEL), lambda i, p, pt: (i, 0, 0))
    return pl.pallas_call(
        functools.partial(_decode_attn_kernel, n_pg=n_pg, lam_init=lam_init),
        grid_spec=pltpu.PrefetchScalarGridSpec(
            num_scalar_prefetch=1,
            grid=(n, n_pages // n_pg),
            in_specs=[one, one, one]
                     + [page_spec((1, D_MODEL, PAGE_SIZE), g) for g in range(n_pg)]
                     + [page_spec((1, PAGE_SIZE * N_HEADS, V_DIM), g) for g in range(n_pg)]
                     + [pl.BlockSpec(lam_p.shape, lambda i, p, pt: (0, 0)),
                        pl.BlockSpec(subln_row.shape, lambda i, p, pt: (0, 0))],
            out_specs=one,
            scratch_shapes=[pltpu.VMEM((N_MAPS, D_MODEL), BF16), pltpu.VMEM((N_MAPS, 1), F32),
                            pltpu.VMEM((N_MAPS, 1), F32), pltpu.VMEM((N_MAPS, D_MODEL), F32)]),
        out_shape=jax.ShapeDtypeStruct((n, 1, D_MODEL), BF16),
        compiler_params=_params("arbitrary", "arbitrary"),
        name="decode_diff_attention",
    )(page_table.reshape(-1), q, k_self, v_self, *([cache_kt] * n_pg), *([cache_v2d] * n_pg), lam_p, subln_row)


def _attn_out_kernel(y_ref, o_ref, w_o_ref, g_post_ref, gm_pre_ref, gm_post_ref, w_up_ref, w_down_ref, out_ref):
    y1 = y_ref[...] + _rms(_dot(o_ref[...], w_o_ref[...]), g_post_ref[...])
    out_ref[...] = _mlp_residual(y1, gm_pre_ref[...], gm_post_ref[...], w_up_ref, w_down_ref)


def _attn_out(y2d, o2d, w_o, g_post, gm_pre, gm_post, w_up, w_down):
    m = y2d.shape[0]
    tm = min(ROW_TILE, m)
    assert m % tm == 0
    row = pl.BlockSpec((tm, D_MODEL), lambda i: (i, 0))
    weights = (w_o, g_post, gm_pre, gm_post, w_up, w_down)
    return pl.pallas_call(
        _attn_out_kernel,
        grid=(m // tm,),
        in_specs=[row, row] + [_resident(a.shape) for a in weights],
        out_specs=row,
        out_shape=jax.ShapeDtypeStruct((m, D_MODEL), F32),
        compiler_params=_params("arbitrary"),
        name="attn_out_mlp",
    )(y2d, o2d, *weights)


def kernel(x_prompt, x_sample, state_conv, state_pool, cache_k, cache_v, page_table, norm_mix_pre, norm_mix_post, norm_mlp_pre, norm_mlp_post, mlp_up, mlp_down, sc_w_in, sc_conv_w, pool_w, pool_scale, sc_w_out, attn_w_qkv, attn_lambda, attn_subln, attn_w_o):
    b, s, _ = x_prompt.shape
    n, t_dec, _ = x_sample.shape
    assert t_dec == 1 and norm_mix_pre.shape[0] == 2
    past_len = page_table.shape[1] * PAGE_SIZE
    lam_init = 0.8 - 0.6 * math.exp(-0.3 * 1)

    def gain(a, layer):
        return a[layer].reshape(1, D_MODEL)

    bf = lambda a: a.astype(BF16)
    l0 = (gain(norm_mix_pre, 0), gain(norm_mix_post, 0), bf(sc_w_in[0]), sc_conv_w[0], bf(pool_w[0]),
          pool_scale[0].reshape(1, D_POOL), bf(sc_w_out[0]), gain(norm_mlp_pre, 0), gain(norm_mlp_post, 0),
          bf(mlp_up[0]), bf(mlp_down[0]))
    l1_out = (bf(attn_w_o[0]), gain(norm_mix_post, 1), gain(norm_mlp_pre, 1), gain(norm_mlp_post, 1),
              bf(mlp_up[1]), bf(mlp_down[1]))
    w_qkv = bf(attn_w_qkv[0])
    w_q, w_k, w_v = w_qkv[:, :D_MODEL], w_qkv[:, D_MODEL:2 * D_MODEL], w_qkv[:, 2 * D_MODEL:]
    g_qkv = gain(norm_mix_pre, 1)
    lam_p = attn_lambda[0]
    subln = attn_subln[0].reshape(1, V_DIM)

    yp, conv_p, pool_p = _mixer0_prompt(x_prompt, *l0)
    ys, conv_s, pool_s = _mixer0_sample(x_sample.reshape(n, D_MODEL),
                                        jnp.swapaxes(state_conv[0], 0, 1), jnp.swapaxes(state_pool[0], 0, 1),
                                        past_len, *l0)

    qtp, ktp, kbp, vp, vtbp = _qkv_prompt(yp, g_qkv, w_q, w_k, w_v)
    qs, ks, vs = _qkv_sample(ys, g_qkv, w_q, w_k, w_v, past_len)
    op = _prompt_attention(qtp, kbp, vtbp, lam_p, subln.reshape(V_DIM, 1), lam_init)
    n_pool = cache_k.shape[1]
    cache_kt = jnp.transpose(cache_k[0], (0, 2, 3, 4, 1)).reshape(n_pool, D_MODEL, PAGE_SIZE)
    cache_v2d = cache_v[0].reshape(n_pool, PAGE_SIZE * N_HEADS, V_DIM)
    os_ = _decode_attention(qs.reshape(n, 1, D_MODEL), ks.reshape(n, 1, D_MODEL), vs.reshape(n, 1, D_MODEL),
                            cache_kt, cache_v2d, page_table, lam_p, jnp.tile(subln, (1, N_HEADS)), lam_init)
    yp = _attn_out(yp.reshape(b * s, D_MODEL), op.reshape(b * s, D_MODEL), *l1_out)
    ys = _attn_out(ys, os_.reshape(n, D_MODEL), *l1_out)

    k_rows_p = jnp.transpose(ktp.reshape(b, N_HEADS, 2, HEAD_DIM, s), (0, 4, 1, 2, 3))
    return (yp.reshape(b, s, D_MODEL), ys.reshape(n, 1, D_MODEL),
            conv_p[None], jnp.swapaxes(conv_s, 0, 1)[None],
            pool_p[None], jnp.swapaxes(pool_s, 0, 1)[None],
            k_rows_p[None], vp.reshape(1, b, s, N_HEADS, V_DIM),
            ks.reshape(1, n, 1, N_HEADS, 2, HEAD_DIM), vs.reshape(1, n, 1, N_HEADS, V_DIM))
```
